```python
import jax, jax.numpy as jnp
from jax import lax
import numpy as np

D_MODEL = 1024
BATCH = 8
SEQ = 2048
DEPTH = 4

CHUNK = 64
QBLK = 128
N_MIXERS = 4
N_MEM = 256

ATT_HEADS = 16
ATT_HD = D_MODEL // ATT_HEADS
ML_HEADS = 4
ML_DK = D_MODEL // 2 // ML_HEADS
ML_DV = D_MODEL // ML_HEADS
GLA_HEADS = 4
GLA_DK = D_MODEL // 2 // GLA_HEADS
GLA_DV = D_MODEL // GLA_HEADS
GLA_RANK = 16
GLA_TAU = 16.0
XA_HEADS = 4
XA_HD = D_MODEL // XA_HEADS
D_FF = 2816
CONV_W = 3

DN_ALPHA = (2.0 * DEPTH) ** 0.25
DN_BETA = (8.0 * DEPTH) ** -0.25
LN_EPS = 1e-5
RMS_EPS = 1e-6
FOX_F_BIAS = 2.0
ML_F_BIAS = 3.0

SB_IN = 3 * D_MODEL
FOX_IN = 3 * D_MODEL + ATT_HEADS
ML_WIDTHS = [ML_HEADS * ML_DK, ML_HEADS * ML_DK, ML_HEADS * ML_DV, ML_HEADS * ML_DV, ML_HEADS, ML_HEADS]
GLA_WIDTHS = [GLA_HEADS * GLA_DK, GLA_HEADS * GLA_DK, GLA_HEADS * GLA_DV, GLA_HEADS * GLA_DV, GLA_RANK]

kernel_name = 'hybrid_sb_fox_mlstm_gla_trunk'

F32 = jnp.float32


def _split_cols(t, widths):
    idx = [int(i) for i in np.cumsum(widths)[:-1]]
    return jnp.split(t, idx, axis=-1)


def _heads(t, n):
    b, s, _ = t.shape
    return t.reshape(b, s, n, -1).transpose(0, 2, 1, 3)


def _merge(t):
    b, n, s, d = t.shape
    return t.transpose(0, 2, 1, 3).reshape(b, s, n * d)


def _to_chunks(t):
    b, hh, s = t.shape[:3]
    return jnp.moveaxis(t.reshape(b, hh, s // CHUNK, CHUNK, *t.shape[3:]), 2, 0)


def _from_chunks(t):
    t = jnp.moveaxis(t, 0, 2)
    return t.reshape(t.shape[0], t.shape[1], -1, *t.shape[4:])


def layer_norm(x, g, b):
    xf = x.astype(F32)
    mu = jnp.mean(xf, -1, keepdims=True)
    var = jnp.mean(jnp.square(xf - mu), -1, keepdims=True)
    return ((xf - mu) * lax.rsqrt(var + LN_EPS) * g + b).astype(x.dtype)


def stick_breaking_core(q, k, v):
    seq = q.shape[2]
    scale = q.shape[-1] ** -0.5
    qf, kf, vf = q.astype(F32), k.astype(F32), v.astype(F32)
    outs = []
    for start in range(0, seq, QBLK):
        end = start + QBLK
        z = jnp.einsum('bhtd,bhsd->bhts', qf[:, :, start:end], kf[:, :, :end]) * scale
        earlier = jnp.arange(end)[None, :] < jnp.arange(start, end)[:, None]
        log_keep = jnp.where(earlier, jax.nn.log_sigmoid(-z), 0.0)
        csum = jnp.cumsum(log_keep, axis=-1)
        log_w = jax.nn.log_sigmoid(z) + (csum[..., -1:] - csum)
        w = jnp.where(earlier, jnp.exp(log_w), 0.0)
        outs.append(jnp.einsum('bhts,bhsd->bhtd', w, vf[:, :, :end]))
    return jnp.concatenate(outs, axis=2).astype(v.dtype)


def mixer_stick_breaking(h, w_in):
    q, k, v = jnp.split(h @ w_in, 3, axis=-1)
    o = stick_breaking_core(_heads(q, ATT_HEADS), _heads(k, ATT_HEADS), _heads(v, ATT_HEADS))
    return _merge(o)


def forgetting_core(q, k, v, log_f):
    seq = q.shape[2]
    scale = q.shape[-1] ** -0.5
    qf, kf, vf = q.astype(F32), k.astype(F32), v.astype(F32)
    cum = jnp.cumsum(log_f.astype(F32), axis=-1)
    outs = []
    for start in range(0, seq, QBLK):
        end = start + QBLK
        z = jnp.einsum('bhtd,bhsd->bhts', qf[:, :, start:end], kf[:, :, :end]) * scale
        z = z + cum[:, :, start:end, None] - cum[:, :, None, :end]
        allowed = jnp.arange(end)[None, :] <= jnp.arange(start, end)[:, None]
        p = jax.nn.softmax(jnp.where(allowed, z, -jnp.inf), axis=-1)
        outs.append(jnp.einsum('bhts,bhsd->bhtd', p, vf[:, :, :end]))
    return jnp.concatenate(outs, axis=2).astype(v.dtype)


def mixer_forgetting(h, w_in, b_f):
    qkv, f_pre = _split_cols(h @ w_in, [3 * D_MODEL, ATT_HEADS])
    q, k, v = jnp.split(qkv, 3, axis=-1)
    log_f = jnp.swapaxes(jax.nn.log_sigmoid((f_pre + b_f).astype(F32)), 1, 2)
    o = forgetting_core(_heads(q, ATT_HEADS), _heads(k, ATT_HEADS), _heads(v, ATT_HEADS), log_f)
    return _merge(o)


def mlstm_core(q, k, v, i_pre, f_pre):
    b, nh, _, dk = q.shape
    dv = v.shape[-1]
    causal = jnp.tril(jnp.ones((CHUNK, CHUNK), dtype=bool))
    xs = (_to_chunks(q.astype(F32)), _to_chunks(k.astype(F32)), _to_chunks(v.astype(F32)),
          _to_chunks(i_pre.astype(F32)), _to_chunks(jax.nn.log_sigmoid(f_pre.astype(F32))))

    def step(carry, blk):
        c_st, n_st, m_st = carry
        qb, kb, vb, ib, lfb = blk
        bcum = jnp.cumsum(lfb, axis=-1)
        d = jnp.where(causal, bcum[..., :, None] - bcum[..., None, :] + ib[..., None, :], -jnp.inf)
        inter = bcum + m_st[..., None]
        m_t = jnp.maximum(inter, jnp.max(d, axis=-1))
        w_intra = jnp.exp(d - m_t[..., None])
        w_inter = jnp.exp(inter - m_t)
        qk = jnp.einsum('bhtd,bhsd->bhts', qb, kb) * w_intra
        num = w_inter[..., None] * jnp.einsum('bhtd,bhde->bhte', qb, c_st) + jnp.einsum('bhts,bhse->bhte', qk, vb)
        den = w_inter * jnp.einsum('bhtd,bhd->bht', qb, n_st) + jnp.sum(qk, axis=-1)
        h_out = num / jnp.maximum(jnp.abs(den), jnp.exp(-m_t))[..., None]
        m_new = m_t[..., -1]
        decay = jnp.exp(bcum[..., -1] + m_st - m_new)
        kw = kb * jnp.exp(bcum[..., -1:] - bcum + ib - m_new[..., None])[..., None]
        c_new = decay[..., None, None] * c_st + jnp.einsum('bhsd,bhse->bhde', kw, vb)
        n_new = decay[..., None] * n_st + jnp.sum(kw, axis=2)
        return (c_new, n_new, m_new), h_out

    init = (jnp.zeros((b, nh, dk, dv), F32), jnp.zeros((b, nh, dk), F32), jnp.zeros((b, nh), F32))
    _, hs = lax.scan(step, init, xs)
    return _from_chunks(hs).astype(v.dtype)


def mixer_mlstm(h, w_in, b_i, b_f):
    q, k, v, o, i_pre, f_pre = _split_cols(h @ w_in, ML_WIDTHS)
    i_pre = jnp.swapaxes((i_pre + b_i).astype(F32), 1, 2)
    f_pre = jnp.swapaxes((f_pre + b_f).astype(F32), 1, 2)
    hh = mlstm_core(_heads(q, ML_HEADS), _heads(k, ML_HEADS) * (ML_DK ** -0.5), _heads(v, ML_HEADS), i_pre, f_pre)
    return _merge(hh) * jax.nn.sigmoid(o)


def gla_core(q, k, v, log_a):
    b, nh, _, dk = q.shape
    dv = v.shape[-1]
    causal = jnp.tril(jnp.ones((CHUNK, CHUNK), dtype=bool))[:, :, None]
    xs = (_to_chunks(q.astype(F32)), _to_chunks(k.astype(F32)), _to_chunks(v.astype(F32)), _to_chunks(log_a.astype(F32)))

    def step(state, blk):
        qb, kb, vb, lab = blk
        bcum = jnp.cumsum(lab, axis=2)
        rel = jnp.where(causal, bcum[:, :, :, None, :] - bcum[:, :, None, :, :], -jnp.inf)
        att = jnp.einsum('bhtd,bhsd,bhtsd->bhts', qb, kb, jnp.exp(rel))
        o = jnp.einsum('bhts,bhse->bhte', att, vb) + jnp.einsum('bhtd,bhde->bhte', qb * jnp.exp(bcum), state)
        blast = bcum[:, :, -1:, :]
        new_state = jnp.exp(blast[:, :, 0, :])[..., None] * state + jnp.einsum('bhsd,bhse->bhde', kb * jnp.exp(blast - bcum), vb)
        return new_state, o

    _, os_ = lax.scan(step, jnp.zeros((b, nh, dk, dv), F32), xs)
    return _from_chunks(os_)


def mixer_gla(h, w_in, w_a2, b_a, norm_g):
    q, k, v, r, a_low = _split_cols(h @ w_in, GLA_WIDTHS)
    log_a = jax.nn.log_sigmoid((a_low @ w_a2 + b_a).astype(F32)) / GLA_TAU
    o = gla_core(_heads(q, GLA_HEADS) * (GLA_DK ** -0.5), _heads(k, GLA_HEADS), _heads(v, GLA_HEADS), _heads(log_a, GLA_HEADS))
    o = o * lax.rsqrt(jnp.mean(jnp.square(o), -1, keepdims=True) + RMS_EPS) * norm_g
    return _merge(o.astype(h.dtype)) * jax.nn.silu(r)


def mem_cross_attn(h, mem, w_q, w_kv, w_o):
    q = _heads(h @ w_q, XA_HEADS).astype(F32)
    k, v = jnp.split(mem @ w_kv, 2, axis=-1)
    k = _heads(k, XA_HEADS).astype(F32)
    v = _heads(v, XA_HEADS).astype(F32)
    p = jax.nn.softmax(jnp.einsum('bhtd,bhmd->bhtm', q, k) * (XA_HD ** -0.5), axis=-1)
    o = jnp.einsum('bhtm,bhmd->bhtd', p, v).astype(h.dtype)
    return _merge(o) @ w_o


def conv_ffn(h, w_up, conv_w, conv_b, w_down):
    u = h @ w_up
    ch = u.shape[-1]
    u = lax.conv_general_dilated(u, conv_w, window_strides=(1,), padding=((CONV_W - 1, 0),),
                                 dimension_numbers=('NWC', 'WIO', 'NWC'), feature_group_count=ch) + conv_b
    g, val = jnp.split(u, 2, axis=-1)
    return (jax.nn.gelu(g) * val) @ w_down


def _dense(key, shape, fan_in, gain=1.0):
    return jax.random.normal(key, shape, F32) * (gain * fan_in ** -0.5)


def setup_inputs(seed: int = 0) -> dict:
    key = jax.random.key(seed)
    ks = iter(jax.random.split(key, 32))
    D = D_MODEL
    n_occ = [len(range(m, DEPTH, N_MIXERS)) for m in range(N_MIXERS)]
    nrm = lambda shape, s: jax.random.normal(next(ks), shape, F32) * s
    return {
        'x': jax.random.normal(next(ks), (BATCH, SEQ, D), F32),
        'mem': jax.random.normal(next(ks), (BATCH, N_MEM, D), F32),
        'ln_g': 1.0 + nrm((DEPTH, 3, D), 0.02),
        'ln_b': nrm((DEPTH, 3, D), 0.02),
        'mix_wo': _dense(next(ks), (DEPTH, D, D), D, DN_BETA),
        'sb_win': _dense(next(ks), (n_occ[0], D, SB_IN), D),
        'fox_win': _dense(next(ks), (n_occ[1], D, FOX_IN), D),
        'fox_bf': FOX_F_BIAS + nrm((n_occ[1], ATT_HEADS), 0.1),
        'ml_win': _dense(next(ks), (n_occ[2], D, sum(ML_WIDTHS)), D),
        'ml_bi': nrm((n_occ[2], ML_HEADS), 0.1),
        'ml_bf': ML_F_BIAS + nrm((n_occ[2], ML_HEADS), 0.1),
        'gla_win': _dense(next(ks), (n_occ[3], D, sum(GLA_WIDTHS)), D),
        'gla_wa2': _dense(next(ks), (n_occ[3], GLA_RANK, GLA_HEADS * GLA_DK), GLA_RANK),
        'gla_ba': nrm((n_occ[3], GLA_HEADS * GLA_DK), 0.1),
        'gla_norm_g': 1.0 + nrm((n_occ[3], GLA_DV), 0.02),
        'xa_wq': _dense(next(ks), (DEPTH, D, D), D),
        'xa_wkv': _dense(next(ks), (DEPTH, D, 2 * D), D),
        'xa_wo': _dense(next(ks), (DEPTH, D, D), D, DN_BETA),
        'ffn_up': _dense(next(ks), (DEPTH, D, 2 * D_FF), D),
        'ffn_conv': _dense(next(ks), (DEPTH, CONV_W, 1, 2 * D_FF), CONV_W),
        'ffn_conv_b': nrm((DEPTH, 2 * D_FF), 0.02),
        'ffn_down': _dense(next(ks), (DEPTH, D_FF, D), D_FF, DN_BETA),
    }


def reference(x, mem, ln_g, ln_b, mix_wo, sb_win, fox_win, fox_bf, ml_win, ml_bi, ml_bf,
              gla_win, gla_wa2, gla_ba, gla_norm_g, xa_wq, xa_wkv, xa_wo,
              ffn_up, ffn_conv, ffn_conv_b, ffn_down):
    h = x
    for layer in range(DEPTH):
        kind = layer % N_MIXERS
        occ = layer // N_MIXERS
        if kind == 0:
            y = mixer_stick_breaking(h, sb_win[occ])
        elif kind == 1:
            y = mixer_forgetting(h, fox_win[occ], fox_bf[occ])
        elif kind == 2:
            y = mixer_mlstm(h, ml_win[occ], ml_bi[occ], ml_bf[occ])
        else:
            y = mixer_gla(h, gla_win[occ], gla_wa2[occ], gla_ba[occ], gla_norm_g[occ])
        h = layer_norm(DN_ALPHA * h + y @ mix_wo[layer], ln_g[layer, 0], ln_b[layer, 0])
        h = layer_norm(DN_ALPHA * h + mem_cross_attn(h, mem, xa_wq[layer], xa_wkv[layer], xa_wo[layer]),
                       ln_g[layer, 1], ln_b[layer, 1])
        h = layer_norm(DN_ALPHA * h + conv_ffn(h, ffn_up[layer], ffn_conv[layer], ffn_conv_b[layer], ffn_down[layer]),
                       ln_g[layer, 2], ln_b[layer, 2])
    return h
```

```python
import functools

import jax
import jax.numpy as jnp
from jax import lax
from jax.experimental import pallas as pl
from jax.experimental.pallas import tpu as pltpu

F32 = jnp.float32
BF16 = jnp.bfloat16

ATT_HEADS = 16
ML_HEADS = 4
GLA_HEADS = 4
GLA_RANK = 16
GLA_TAU = 16.0
XA_HEADS = 4
CHUNK = 64
CONV_W = 3
LN_EPS = 1e-5
RMS_EPS = 1e-6

LANES = 128
BF16_SUBLANES = 16
VMEM_LIMIT = 48 * 1024 * 1024

_NT = (((1,), (1,)), ((), ()))
_TN = (((0,), (0,)), ((), ()))


def _params(*sem):
    return pltpu.CompilerParams(dimension_semantics=sem, vmem_limit_bytes=VMEM_LIMIT)


def _dot(a, b):
    return jnp.dot(a, b, preferred_element_type=F32)


def _dot_nt(a, b):
    return lax.dot_general(a, b, _NT, preferred_element_type=F32)


def _dot_tn(a, b):
    return lax.dot_general(a, b, _TN, preferred_element_type=F32)


def _log_sigmoid(x):
    return jnp.minimum(x, 0.0) - jnp.log1p(jnp.exp(-jnp.abs(x)))


def _sigmoid(x):
    return 1.0 / (1.0 + jnp.exp(-x))


def _split2(x):
    hi = x.astype(BF16)
    lo = (x - hi.astype(F32)).astype(BF16)
    return hi, lo


def _split3(x):
    hi = x.astype(BF16)
    r = x - hi.astype(F32)
    mid = r.astype(BF16)
    lo = (r - mid.astype(F32)).astype(BF16)
    return hi, mid, lo


def _layer_norm(z, g, b):
    mu = jnp.mean(z, axis=-1, keepdims=True)
    zc = z - mu
    var = jnp.mean(zc * zc, axis=-1, keepdims=True)
    return zc * lax.rsqrt(var + LN_EPS) * g + b


def _linear_kernel(x_ref, w_ref, o_ref):
    o_ref[...] = _dot(x_ref[...], w_ref[...]).astype(o_ref.dtype)


def linear(x, w, out_dtype, tm, tn):
    m, k = x.shape
    n = w.shape[1]
    tm = min(tm, m)
    tn = min(tn, n)
    return pl.pallas_call(
        _linear_kernel,
        grid=(m // tm, n // tn),
        in_specs=[pl.BlockSpec((tm, k), lambda i, j: (i, 0)),
                  pl.BlockSpec((k, tn), lambda i, j: (0, j))],
        out_specs=pl.BlockSpec((tm, tn), lambda i, j: (i, j)),
        out_shape=jax.ShapeDtypeStruct((m, n), out_dtype),
        compiler_params=_params("parallel", "arbitrary"),
        name="linear",
    )(x, w)


def _linear_res_ln_kernel(y_ref, w_ref, h_ref, g_ref, b_ref, of_ref, ob_ref, *, alpha):
    z = alpha * h_ref[...] + _dot(y_ref[...], w_ref[...])
    o = _layer_norm(z, g_ref[...], b_ref[...])
    of_ref[...] = o
    ob_ref[...] = o.astype(BF16)


def linear_res_ln(y, w, h, g, b, alpha, tm):
    m, k = y.shape
    d = w.shape[1]
    tm = min(tm, m)
    return pl.pallas_call(
        functools.partial(_linear_res_ln_kernel, alpha=alpha),
        grid=(m // tm,),
        in_specs=[pl.BlockSpec((tm, k), lambda i: (i, 0)),
                  pl.BlockSpec((k, d), lambda i: (0, 0)),
                  pl.BlockSpec((tm, d), lambda i: (i, 0)),
                  pl.BlockSpec((1, d), lambda i: (0, 0)),
                  pl.BlockSpec((1, d), lambda i: (0, 0))],
        out_specs=[pl.BlockSpec((tm, d), lambda i: (i, 0)),
                   pl.BlockSpec((tm, d), lambda i: (i, 0))],
        out_shape=[jax.ShapeDtypeStruct((m, d), F32),
                   jax.ShapeDtypeStruct((m, d), BF16)],
        compiler_params=_params("parallel"),
        name="linear_res_ln",
    )(y, w, h, g, b)


def _sb_kernel(q_ref, k_ref, v_ref, o_ref, *, blk, hd):
    i = pl.program_id(2)
    lane = lax.broadcasted_iota(jnp.int32, (blk, LANES), 1)
    qs = q_ref[...] * (hd ** -0.5)
    zero = jnp.zeros_like(qs)
    q_heads = (jnp.where(lane < hd, qs, zero), jnp.where(lane >= hd, qs, zero))
    row = lax.broadcasted_iota(jnp.int32, (blk, blk), 0)
    col = lax.broadcasted_iota(jnp.int32, (blk, blk), 1)
    earlier = col < row
    later = jnp.where(row > col, 1.0, 0.0).astype(BF16)

    def block(kb, carry, diag):
        start = pl.multiple_of(kb * blk, blk)
        ks = k_ref[pl.ds(start, blk), :]
        vs = v_ref[pl.ds(start, blk), :]
        new = []
        for hh in range(2):
            run, acc = carry[hh]
            z = _dot_nt(q_heads[hh], ks)
            sp = jnp.maximum(z, 0.0) + jnp.log1p(jnp.exp(-jnp.abs(z)))
            log_keep = -sp
            if diag:
                log_keep = jnp.where(earlier, log_keep, 0.0)
            hi, lo = _split2(log_keep)
            suffix = _dot(hi, later) + _dot(lo, later)
            log_w = (z - sp) + suffix + run
            w = jnp.exp(log_w)
            if diag:
                w = jnp.where(earlier, w, 0.0)
            acc = acc + _dot(w.astype(BF16), vs)
            run = run + suffix[:, 0:1] + log_keep[:, 0:1]
            new.append((run, acc))
        return tuple(new)

    init = tuple((jnp.zeros((blk, 1), F32), jnp.zeros((blk, LANES), F32)) for _ in range(2))
    carry = block(i, init, True)
    carry = lax.fori_loop(0, i, lambda j, c: block(i - 1 - j, c, False), carry)
    o_ref[...] = jnp.where(lane < hd, carry[0][1], carry[1][1]).astype(o_ref.dtype)


def sb_attention(qkv, batch, seq, d_model, blk=128):
    hd = d_model // ATT_HEADS
    npair = d_model // LANES
    nq = seq // blk
    return pl.pallas_call(
        functools.partial(_sb_kernel, blk=blk, hd=hd),
        grid=(batch, npair, nq),
        in_specs=[pl.BlockSpec((blk, LANES), lambda b, p, i: (b * nq + i, p)),
                  pl.BlockSpec((seq, LANES), lambda b, p, i: (b, npair + p)),
                  pl.BlockSpec((seq, LANES), lambda b, p, i: (b, 2 * npair + p))],
        out_specs=pl.BlockSpec((blk, LANES), lambda b, p, i: (b * nq + i, p)),
        out_shape=jax.ShapeDtypeStruct((batch * seq, d_model), BF16),
        compiler_params=_params("parallel", "parallel", "arbitrary"),
        name="sb_attention",
    )(qkv, qkv, qkv)


def _fox_cum_kernel(f_ref, b_ref, o_ref, *, seq, blk):
    row = lax.broadcasted_iota(jnp.int32, (blk, blk), 0)
    col = lax.broadcasted_iota(jnp.int32, (blk, blk), 1)
    tri = jnp.where(col <= row, 1.0, 0.0).astype(BF16)
    run = jnp.zeros((1, f_ref.shape[-1]), F32)
    for c in range(seq // blk):
        lf = _log_sigmoid(f_ref[0, c * blk:(c + 1) * blk, :] + b_ref[...])
        hi, mid, lo = _split3(lf)
        cs = _dot(tri, hi) + _dot(tri, mid) + _dot(tri, lo) + run
        o_ref[0, c * blk:(c + 1) * blk, :] = cs
        run = cs[blk - 1:blk, :]


def fox_cum(f_pre, b_f, blk=256):
    batch, seq, nh = f_pre.shape
    blk = min(blk, seq)
    return pl.pallas_call(
        functools.partial(_fox_cum_kernel, seq=seq, blk=blk),
        grid=(batch,),
        in_specs=[pl.BlockSpec((1, seq, nh), lambda b: (b, 0, 0)),
                  pl.BlockSpec((1, nh), lambda b: (0, 0))],
        out_specs=pl.BlockSpec((1, seq, nh), lambda b: (b, 0, 0)),
        out_shape=jax.ShapeDtypeStruct((batch, seq, nh), F32),
        compiler_params=_params("parallel"),
        name="fox_cum",
    )(f_pre, b_f)


def _fox_kernel(q_ref, k_ref, v_ref, cc_ref, cr_ref, o_ref, *, blk, hd):
    i = pl.program_id(2)
    lane = lax.broadcasted_iota(jnp.int32, (blk, LANES), 1)
    qs = q_ref[...] * (hd ** -0.5)
    zero = jnp.zeros_like(qs)
    q_heads = (jnp.where(lane < hd, qs, zero), jnp.where(lane >= hd, qs, zero))
    row = lax.broadcasted_iota(jnp.int32, (blk, blk), 0)
    col = lax.broadcasted_iota(jnp.int32, (blk, blk), 1)
    allowed = col <= row
    cum_col = cc_ref[0, 0]

    def block(kb, carry, diag):
        start = pl.multiple_of(kb * blk, blk)
        ks = k_ref[pl.ds(start, blk), :]
        vs = v_ref[pl.ds(start, blk), :]
        new = []
        for hh in range(2):
            m, l, acc = carry[hh]
            cum_row = cr_ref[0, 0, hh:hh + 1, pl.ds(start, blk)]
            z = _dot_nt(q_heads[hh], ks) + (cum_col[:, hh:hh + 1] - cum_row)
            if diag:
                z = jnp.where(allowed, z, -jnp.inf)
            m_new = jnp.maximum(m, jnp.max(z, axis=-1, keepdims=True))
            a = jnp.exp(m - m_new)
            p = jnp.exp(z - m_new)
            l = a * l + jnp.sum(p, axis=-1, keepdims=True)
            acc = a * acc + _dot(p.astype(BF16), vs)
            new.append((m_new, l, acc))
        return tuple(new)

    init = tuple((jnp.full((blk, 1), -jnp.inf, F32), jnp.zeros((blk, 1), F32),
                  jnp.zeros((blk, LANES), F32)) for _ in range(2))
    carry = block(i, init, True)
    carry = lax.fori_loop(0, i, lambda j, c: block(j, c, False), carry)
    o0 = carry[0][2] / carry[0][1]
    o1 = carry[1][2] / carry[1][1]
    o_ref[...] = jnp.where(lane < hd, o0, o1).astype(o_ref.dtype)


def fox_attention(qkv, cum, batch, seq, d_model, blk=128):
    hd = d_model // ATT_HEADS
    npair = d_model // LANES
    nq = seq // blk
    cum_col = cum.reshape(batch, seq, npair, 2).transpose(0, 2, 1, 3)
    cum_row = cum_col.transpose(0, 1, 3, 2)
    return pl.pallas_call(
        functools.partial(_fox_kernel, blk=blk, hd=hd),
        grid=(batch, npair, nq),
        in_specs=[pl.BlockSpec((blk, LANES), lambda b, p, i: (b * nq + i, p)),
                  pl.BlockSpec((seq, LANES), lambda b, p, i: (b, npair + p)),
                  pl.BlockSpec((seq, LANES), lambda b, p, i: (b, 2 * npair + p)),
                  pl.BlockSpec((1, 1, blk, 2), lambda b, p, i: (b, p, i, 0)),
                  pl.BlockSpec((1, 1, 2, seq), lambda b, p, i: (b, p, 0, 0))],
        out_specs=pl.BlockSpec((blk, LANES), lambda b, p, i: (b * nq + i, p)),
        out_shape=jax.ShapeDtypeStruct((batch * seq, d_model), BF16),
        compiler_params=_params("parallel", "parallel", "arbitrary"),
        name="fox_attention",
    )(qkv, qkv, qkv, cum_col, cum_row)


def _mlstm_kernel(q_ref, k_ref, v_ref, og_ref, gc_ref, gr_ref, o_ref, c_ref, n_ref, m_ref,
                  *, nh, dk, dv, L):
    @pl.when(pl.program_id(1) == 0)
    def _():
        c_ref[...] = jnp.zeros_like(c_ref)
        n_ref[...] = jnp.zeros_like(n_ref)
        m_ref[...] = jnp.zeros_like(m_ref)

    row = lax.broadcasted_iota(jnp.int32, (L, L), 0)
    col = lax.broadcasted_iota(jnp.int32, (L, L), 1)
    causal = col <= row
    gcol = gc_ref[0]
    grow = gr_ref[0, 0]
    scale = dk ** -0.5
    for h in range(nh):
        i_col = gcol[:, h:h + 1]
        i_row = grow[h:h + 1, :]
        lf_col = _log_sigmoid(gcol[:, nh + h:nh + h + 1])
        lf_row = _log_sigmoid(grow[nh + h:nh + h + 1, :])
        bcum_col = jnp.sum(jnp.where(causal, lf_row, 0.0), axis=1, keepdims=True)
        bcum_row = jnp.sum(jnp.where(row <= col, lf_col, 0.0), axis=0, keepdims=True)
        m_prev = m_ref[h]
        d = jnp.where(causal, bcum_col - bcum_row + i_row, -jnp.inf)
        inter = bcum_col + m_prev
        m_t = jnp.maximum(inter, jnp.max(d, axis=1, keepdims=True))
        w_intra = jnp.exp(d - m_t)
        w_inter = jnp.exp(inter - m_t)
        qh = q_ref[:, h * dk:(h + 1) * dk]
        kh = k_ref[:, h * dk:(h + 1) * dk]
        vh = v_ref[:, h * dv:(h + 1) * dv]
        qk = _dot_nt(qh, kh) * (scale * w_intra)
        n_row = n_ref[h]
        num = w_inter * _dot(qh, c_ref[h].astype(BF16)) + _dot(qk.astype(BF16), vh)
        den = (w_inter * jnp.sum(qh.astype(F32) * n_row, axis=1, keepdims=True)
               + jnp.sum(qk, axis=1, keepdims=True))
        h_out = num / jnp.maximum(jnp.abs(den), jnp.exp(-m_t))
        m_new = m_t[L - 1:L, :]
        blast = bcum_col[L - 1:L, :]
        decay = jnp.exp(blast + m_prev - m_new)
        kw = (kh.astype(F32) * scale) * jnp.exp(blast - bcum_col + i_col - m_new)
        c_ref[h] = decay * c_ref[h] + _dot_tn(kw.astype(BF16), vh)
        n_ref[h] = decay * n_row + jnp.sum(kw, axis=0, keepdims=True)
        m_ref[h] = m_new
        gate = _sigmoid(og_ref[:, h * dv:(h + 1) * dv].astype(F32))
        o_ref[:, h * dv:(h + 1) * dv] = (h_out * gate).astype(o_ref.dtype)


def mlstm_core(proj, gates, batch, seq, d_model):
    nh, L = ML_HEADS, CHUNK
    dk = d_model // 2 // nh
    dv = d_model // nh
    nc = seq // L
    qw = nh * dk
    gates_row = gates.reshape(batch, nc, L, 2 * nh).transpose(0, 1, 3, 2)
    return pl.pallas_call(
        functools.partial(_mlstm_kernel, nh=nh, dk=dk, dv=dv, L=L),
        grid=(batch, nc),
        in_specs=[pl.BlockSpec((L, qw), lambda b, c: (b * nc + c, 0)),
                  pl.BlockSpec((L, qw), lambda b, c: (b * nc + c, 1)),
                  pl.BlockSpec((L, d_model), lambda b, c: (b * nc + c, 1)),
                  pl.BlockSpec((L, d_model), lambda b, c: (b * nc + c, 2)),
                  pl.BlockSpec((1, L, 2 * nh), lambda b, c: (b, c, 0)),
                  pl.BlockSpec((1, 1, 2 * nh, L), lambda b, c: (b, c, 0, 0))],
        out_specs=pl.BlockSpec((L, d_model), lambda b, c: (b * nc + c, 0)),
        out_shape=jax.ShapeDtypeStruct((batch * seq, d_model), BF16),
        scratch_shapes=[pltpu.VMEM((nh, dk, dv), F32),
                        pltpu.VMEM((nh, 1, dk), F32),
                        pltpu.VMEM((nh, 1, 1), F32)],
        compiler_params=_params("parallel", "arbitrary"),
        name="mlstm_core",
    )(proj, proj, proj, proj, gates, gates_row)


def _gla_kernel(q_ref, k_ref, v_ref, r_ref, a_ref, wh_ref, wl_ref, ba_ref, ng_ref, o_ref, st_ref,
                *, nh, dk, dv, L):
    @pl.when(pl.program_id(1) == 0)
    def _():
        st_ref[...] = jnp.zeros_like(st_ref)

    row = lax.broadcasted_iota(jnp.int32, (L, L), 0)
    col = lax.broadcasted_iota(jnp.int32, (L, L), 1)
    causal = col <= row
    tri = jnp.where(causal, 1.0, 0.0).astype(BF16)
    a_hi, a_lo = _split2(a_ref[...])
    scale = dk ** -0.5
    for h in range(nh):
        hs = slice(h * dk, (h + 1) * dk)
        wh = wh_ref[:, hs]
        logit = _dot(a_hi, wh) + _dot(a_lo, wh) + _dot(a_hi, wl_ref[:, hs]) + ba_ref[:, hs]
        la = _log_sigmoid(logit) * (1.0 / GLA_TAU)
        la_hi, la_lo = _split2(la)
        bcum = _dot(tri, la_hi) + _dot(tri, la_lo)
        blast = bcum[L - 1:L, :]
        ref = bcum[L // 2 - 1:L // 2, :]
        qf = q_ref[:, hs].astype(F32) * scale
        kf = k_ref[:, hs].astype(F32)
        vh = v_ref[:, h * dv:(h + 1) * dv]
        q_rel = (qf * jnp.exp(bcum - ref)).astype(BF16)
        k_rel = (kf * jnp.exp(ref - bcum)).astype(BF16)
        q_dec = (qf * jnp.exp(bcum)).astype(BF16)
        k_dec = (kf * jnp.exp(blast - bcum)).astype(BF16)
        att = jnp.where(causal, _dot_nt(q_rel, k_rel), 0.0)
        st = st_ref[h]
        o = _dot(att.astype(BF16), vh) + _dot_nt(q_dec, st.astype(BF16))
        st_ref[h] = st * jnp.exp(blast) + _dot_tn(vh, k_dec)
        o = o * lax.rsqrt(jnp.mean(o * o, axis=-1, keepdims=True) + RMS_EPS) * ng_ref[...]
        r = r_ref[:, h * dv:(h + 1) * dv].astype(F32)
        o_ref[:, h * dv:(h + 1) * dv] = (o * (r * _sigmoid(r))).astype(o_ref.dtype)


def gla_core(proj, a_low, wa_hi, wa_lo, b_a, norm_g, batch, seq, d_model):
    nh, L = GLA_HEADS, CHUNK
    dk = d_model // 2 // nh
    dv = d_model // nh
    nc = seq // L
    qw = nh * dk
    return pl.pallas_call(
        functools.partial(_gla_kernel, nh=nh, dk=dk, dv=dv, L=L),
        grid=(batch, nc),
        in_specs=[pl.BlockSpec((L, qw), lambda b, c: (b * nc + c, 0)),
                  pl.BlockSpec((L, qw), lambda b, c: (b * nc + c, 1)),
                  pl.BlockSpec((L, d_model), lambda b, c: (b * nc + c, 1)),
                  pl.BlockSpec((L, d_model), lambda b, c: (b * nc + c, 2)),
                  pl.BlockSpec((L, LANES), lambda b, c: (b * nc + c, 0)),
                  pl.BlockSpec((LANES, qw), lambda b, c: (0, 0)),
                  pl.BlockSpec((LANES, qw), lambda b, c: (0, 0)),
                  pl.BlockSpec((1, qw), lambda b, c: (0, 0)),
                  pl.BlockSpec((1, dv), lambda b, c: (0, 0))],
        out_specs=pl.BlockSpec((L, d_model), lambda b, c: (b * nc + c, 0)),
        out_shape=jax.ShapeDtypeStruct((batch * seq, d_model), BF16),
        scratch_shapes=[pltpu.VMEM((nh, dv, dk), F32)],
        compiler_params=_params("parallel", "arbitrary"),
        name="gla_core",
    )(proj, proj, proj, proj, a_low, wa_hi, wa_lo, b_a, norm_g)


def _xattn_kernel(hb_ref, hf_ref, wq_ref, kv_ref, wo_ref, g_ref, b_ref, of_ref, ob_ref,
                  *, nh, alpha):
    d = wq_ref.shape[1]
    hd = d // nh
    q = (_dot(hb_ref[...], wq_ref[...]) * (hd ** -0.5)).astype(BF16)
    outs = []
    for h in range(nh):
        kh = kv_ref[:, h * hd:(h + 1) * hd]
        vh = kv_ref[:, d + h * hd:d + (h + 1) * hd]
        s = _dot_nt(q[:, h * hd:(h + 1) * hd], kh)
        p = jnp.exp(s - jnp.max(s, axis=-1, keepdims=True))
        l = jnp.sum(p, axis=-1, keepdims=True)
        outs.append((_dot(p.astype(BF16), vh) / l).astype(BF16))
    o = jnp.concatenate(outs, axis=-1)
    z = alpha * hf_ref[...] + _dot(o, wo_ref[...])
    y = _layer_norm(z, g_ref[...], b_ref[...])
    of_ref[...] = y
    ob_ref[...] = y.astype(BF16)


def xattn_block(hb, hf, wq, kv, wo, g, b, batch, seq, n_mem, alpha, tm):
    d = hb.shape[1]
    tm = min(tm, seq)
    nt = seq // tm
    return pl.pallas_call(
        functools.partial(_xattn_kernel, nh=XA_HEADS, alpha=alpha),
        grid=(batch, nt),
        in_specs=[pl.BlockSpec((tm, d), lambda bi, i: (bi * nt + i, 0)),
                  pl.BlockSpec((tm, d), lambda bi, i: (bi * nt + i, 0)),
                  pl.BlockSpec((d, d), lambda bi, i: (0, 0)),
                  pl.BlockSpec((n_mem, 2 * d), lambda bi, i: (bi, 0)),
                  pl.BlockSpec((d, d), lambda bi, i: (0, 0)),
                  pl.BlockSpec((1, d), lambda bi, i: (0, 0)),
                  pl.BlockSpec((1, d), lambda bi, i: (0, 0))],
        out_specs=[pl.BlockSpec((tm, d), lambda bi, i: (bi * nt + i, 0)),
                   pl.BlockSpec((tm, d), lambda bi, i: (bi * nt + i, 0))],
        out_shape=[jax.ShapeDtypeStruct((batch * seq, d), F32),
                   jax.ShapeDtypeStruct((batch * seq, d), BF16)],
        compiler_params=_params("parallel", "parallel"),
        name="xattn_block",
    )(hb, hf, wq, kv, wo, g, b)


def _gelu_tanh(x):
    return 0.5 * x * (1.0 + jnp.tanh(0.7978845608028654 * (x + 0.044715 * (x * x * x))))


def _ffn_kernel(hb_ref, hp_ref, hf_ref, wg_ref, wv_ref, cg_ref, cv_ref, bg_ref, bv_ref, wd_ref,
                g_ref, b_ref, of_ref, ob_ref, acc_ref, *, tm, blocks_per_seq, alpha):
    i = pl.program_id(0)
    j = pl.program_id(1)

    @pl.when(j == 0)
    def _():
        acc_ref[...] = jnp.zeros_like(acc_ref)

    hb = hb_ref[...]
    hp = hp_ref[...]
    hp = jnp.where(i % blocks_per_seq == 0, jnp.zeros_like(hp), hp)
    halo = hp.shape[0]
    tf = wg_ref.shape[1]
    row = lax.broadcasted_iota(jnp.int32, (tm, tf), 0)

    def branch(w_ref, c_ref, bias_ref):
        w = w_ref[...]
        u = _dot(hb, w)
        up = _dot(hp, w)
        p1 = up[halo - 1:halo, :]
        p2 = up[halo - 2:halo - 1, :]
        u1 = jnp.where(row == 0, p1, pltpu.roll(u, 1, 0))
        u2 = jnp.where(row == 0, p2, jnp.where(row == 1, p1, pltpu.roll(u, 2, 0)))
        c = c_ref[...]
        return c[0:1, :] * u2 + c[1:2, :] * u1 + c[2:3, :] * u + bias_ref[...]

    gate = branch(wg_ref, cg_ref, bg_ref)
    val = branch(wv_ref, cv_ref, bv_ref)
    act = (_gelu_tanh(gate) * val).astype(BF16)
    acc_ref[...] += _dot(act, wd_ref[...])

    @pl.when(j == pl.num_programs(1) - 1)
    def _():
        y = _layer_norm(alpha * hf_ref[...] + acc_ref[...], g_ref[...], b_ref[...])
        of_ref[...] = y
        ob_ref[...] = y.astype(BF16)


def ffn_block(hb, hf, w_up, conv_w, conv_b, w_down, g, b, seq, alpha, tm, tf):
    t, d = hb.shape
    f = w_down.shape[0]
    tm = min(tm, seq)
    nf = f // tf
    halo = BF16_SUBLANES
    hpb = tm // halo
    return pl.pallas_call(
        functools.partial(_ffn_kernel, tm=tm, blocks_per_seq=seq // tm, alpha=alpha),
        grid=(t // tm, nf),
        in_specs=[pl.BlockSpec((tm, d), lambda i, j: (i, 0)),
                  pl.BlockSpec((halo, d), lambda i, j: (jnp.maximum(i * hpb - 1, 0), 0)),
                  pl.BlockSpec((tm, d), lambda i, j: (i, 0)),
                  pl.BlockSpec((d, tf), lambda i, j: (0, j)),
                  pl.BlockSpec((d, tf), lambda i, j: (0, nf + j)),
                  pl.BlockSpec((CONV_W, tf), lambda i, j: (0, j)),
                  pl.BlockSpec((CONV_W, tf), lambda i, j: (0, nf + j)),
                  pl.BlockSpec((1, tf), lambda i, j: (0, j)),
                  pl.BlockSpec((1, tf), lambda i, j: (0, nf + j)),
                  pl.BlockSpec((tf, d), lambda i, j: (j, 0)),
                  pl.BlockSpec((1, d), lambda i, j: (0, 0)),
                  pl.BlockSpec((1, d), lambda i, j: (0, 0))],
        out_specs=[pl.BlockSpec((tm, d), lambda i, j: (i, 0)),
                   pl.BlockSpec((tm, d), lambda i, j: (i, 0))],
        out_shape=[jax.ShapeDtypeStruct((t, d), F32),
                   jax.ShapeDtypeStruct((t, d), BF16)],
        scratch_shapes=[pltpu.VMEM((tm, d), F32)],
        compiler_params=_params("parallel", "arbitrary"),
        name="ffn_block",
    )(hb, hb, hf, w_up, w_up, conv_w, conv_w, conv_b, conv_b, w_down, g, b)


def _pad_cols(w, width):
    return jnp.pad(w, ((0, 0), (0, width - w.shape[1])))


def kernel(x, mem, ln_g, ln_b, mix_wo, sb_win, fox_win, fox_bf, ml_win, ml_bi, ml_bf,
           gla_win, gla_wa2, gla_ba, gla_norm_g, xa_wq, xa_wkv, xa_wo,
           ffn_up, ffn_conv, ffn_conv_b, ffn_down):
    batch, seq, d = x.shape
    n_mem = mem.shape[1]
    depth = mix_wo.shape[0]
    t = batch * seq
    alpha = (2.0 * depth) ** 0.25
    main = 3 * d

    hf = x.reshape(t, d)
    hb = hf.astype(BF16)
    memb = mem.reshape(batch * n_mem, d).astype(BF16)

    for layer in range(depth):
        kind = layer % 4
        occ = layer // 4
        w_in = (sb_win, fox_win, ml_win, gla_win)[kind][occ]
        proj = linear(hb, w_in[:, :main].astype(BF16), BF16, tm=1024, tn=1024)
        if kind == 0:
            y = sb_attention(proj, batch, seq, d)
        elif kind == 1:
            f_pre = linear(hb, _pad_cols(w_in[:, main:], LANES).astype(BF16), F32, tm=1024, tn=LANES)
            cum = fox_cum(f_pre.reshape(batch, seq, LANES), _pad_cols(fox_bf[occ][None, :], LANES))
            y = fox_attention(proj, cum[:, :, :ATT_HEADS], batch, seq, d)
        elif kind == 2:
            g_pre = linear(hb, _pad_cols(w_in[:, main:], LANES).astype(BF16), F32, tm=1024, tn=LANES)
            gates = g_pre[:, :2 * ML_HEADS] + jnp.concatenate([ml_bi[occ], ml_bf[occ]])[None, :]
            y = mlstm_core(proj, gates.reshape(batch, seq, 2 * ML_HEADS), batch, seq, d)
        else:
            a_low = linear(hb, _pad_cols(w_in[:, main:], LANES).astype(BF16), F32, tm=1024, tn=LANES)
            wa = jnp.pad(gla_wa2[occ], ((0, LANES - GLA_RANK), (0, 0)))
            wa_hi = wa.astype(BF16)
            wa_lo = (wa - wa_hi.astype(F32)).astype(BF16)
            y = gla_core(proj, a_low, wa_hi, wa_lo, gla_ba[occ][None, :], gla_norm_g[occ][None, :],
                         batch, seq, d)
        hf, hb = linear_res_ln(y, mix_wo[layer].astype(BF16), hf, ln_g[layer, 0][None, :],
                               ln_b[layer, 0][None, :], alpha, tm=512)

        kv = linear(memb, xa_wkv[layer].astype(BF16), BF16, tm=1024, tn=1024)
        hf, hb = xattn_block(hb, hf, xa_wq[layer].astype(BF16), kv, xa_wo[layer].astype(BF16),
                             ln_g[layer, 1][None, :], ln_b[layer, 1][None, :],
                             batch, seq, n_mem, alpha, tm=512)

        hf, hb = ffn_block(hb, hf, ffn_up[layer].astype(BF16), ffn_conv[layer, :, 0, :],
                           ffn_conv_b[layer][None, :], ffn_down[layer].astype(BF16),
                           ln_g[layer, 2][None, :], ln_b[layer, 2][None, :],
                           seq, alpha, tm=512, tf=256)
    return hf.reshape(batch, seq, d)
```

```python
import functools

import jax
import jax.numpy as jnp
from jax import lax
from jax.experimental import pallas as pl
from jax.experimental.pallas import tpu as pltpu

F32 = jnp.float32
BF16 = jnp.bfloat16

ATT_HEADS = 16
ML_HEADS = 4
GLA_HEADS = 4
GLA_RANK = 16
GLA_TAU = 16.0
XA_HEADS = 4
CHUNK = 64
CONV_W = 3
LN_EPS = 1e-5
RMS_EPS = 1e-6

LANES = 128
BF16_SUBLANES = 16
VMEM_LIMIT = 48 * 1024 * 1024

_NT = (((1,), (1,)), ((), ()))
_TN = (((0,), (0,)), ((), ()))


def _params(*sem):
    return pltpu.CompilerParams(dimension_semantics=sem, vmem_limit_bytes=VMEM_LIMIT)


def _dot(a, b):
    return jnp.dot(a, b, preferred_element_type=F32)


def _dot_nt(a, b):
    return lax.dot_general(a, b, _NT, preferred_element_type=F32)


def _dot_tn(a, b):
    return lax.dot_general(a, b, _TN, preferred_element_type=F32)


def _log_sigmoid(x):
    return jnp.minimum(x, 0.0) - jnp.log1p(jnp.exp(-jnp.abs(x)))


def _sigmoid(x):
    return 1.0 / (1.0 + jnp.exp(-x))


def _split2(x):
    hi = x.astype(BF16)
    lo = (x - hi.astype(F32)).astype(BF16)
    return hi, lo


def _split3(x):
    hi = x.astype(BF16)
    r = x - hi.astype(F32)
    mid = r.astype(BF16)
    lo = (r - mid.astype(F32)).astype(BF16)
    return hi, mid, lo


def _layer_norm(z, g, b):
    mu = jnp.mean(z, axis=-1, keepdims=True)
    zc = z - mu
    var = jnp.mean(zc * zc, axis=-1, keepdims=True)
    return zc * lax.rsqrt(var + LN_EPS) * g + b


def _linear_kernel(x_ref, w_ref, o_ref):
    o_ref[...] = _dot(x_ref[...], w_ref[...]).astype(o_ref.dtype)


def linear(x, w, out_dtype, tm, tn):
    m, k = x.shape
    n = w.shape[1]
    tm = min(tm, m)
    tn = min(tn, n)
    return pl.pallas_call(
        _linear_kernel,
        grid=(m // tm, n // tn),
        in_specs=[pl.BlockSpec((tm, k), lambda i, j: (i, 0)),
                  pl.BlockSpec((k, tn), lambda i, j: (0, j))],
        out_specs=pl.BlockSpec((tm, tn), lambda i, j: (i, j)),
        out_shape=jax.ShapeDtypeStruct((m, n), out_dtype),
        compiler_params=_params("parallel", "arbitrary"),
        name="linear",
    )(x, w)


def _linear_res_ln_kernel(y_ref, w_ref, h_ref, g_ref, b_ref, of_ref, ob_ref, *, alpha):
    z = alpha * h_ref[...] + _dot(y_ref[...], w_ref[...])
    o = _layer_norm(z, g_ref[...], b_ref[...])
    of_ref[...] = o
    ob_ref[...] = o.astype(BF16)


def linear_res_ln(y, w, h, g, b, alpha, tm):
    m, k = y.shape
    d = w.shape[1]
    tm = min(tm, m)
    return pl.pallas_call(
        functools.partial(_linear_res_ln_kernel, alpha=alpha),
        grid=(m // tm,),
        in_specs=[pl.BlockSpec((tm, k), lambda i: (i, 0)),
                  pl.BlockSpec((k, d), lambda i: (0, 0)),
                  pl.BlockSpec((tm, d), lambda i: (i, 0)),
                  pl.BlockSpec((1, d), lambda i: (0, 0)),
                  pl.BlockSpec((1, d), lambda i: (0, 0))],
        out_specs=[pl.BlockSpec((tm, d), lambda i: (i, 0)),
                   pl.BlockSpec((tm, d), lambda i: (i, 0))],
        out_shape=[jax.ShapeDtypeStruct((m, d), F32),
                   jax.ShapeDtypeStruct((m, d), BF16)],
        compiler_params=_params("parallel"),
        name="linear_res_ln",
    )(y, w, h, g, b)


def _head_queries(q_ref, tq, hd):
    lane = lax.broadcasted_iota(jnp.int32, (tq, LANES), 1)
    qs = q_ref[...] * (hd ** -0.5)
    zero = jnp.zeros_like(qs)
    return lane, (jnp.where(lane < hd, qs, zero), jnp.where(lane >= hd, qs, zero))


def _sweep(i, tq, tk, block, masks, ascending):
    nd = tq // tk
    nfull = i * nd

    def diagonal():
        for c in (range(nd) if ascending else reversed(range(nd))):
            block(pl.multiple_of(i * tq + c * tk, tk), c * tk, masks[c])

    def full_body(j, carry):
        kb = j if ascending else nfull - 1 - j
        block(pl.multiple_of(kb * tk, tk), 0, None)
        return carry

    if ascending:
        lax.fori_loop(0, nfull, full_body, 0)
        diagonal()
    else:
        diagonal()
        lax.fori_loop(0, nfull, full_body, 0)


def _diag_masks(tq, tk, inclusive):
    masks = []
    for c in range(tq // tk):
        n = tq - c * tk
        col = lax.broadcasted_iota(jnp.int32, (n, tk), 1)
        row = lax.broadcasted_iota(jnp.int32, (n, tk), 0)
        masks.append(col <= row if inclusive else col < row)
    return masks


def _sb_kernel(q_ref, k_ref, v_ref, o_ref, acc_ref, run_ref, *, tq, tk, hd):
    i = pl.program_id(2)
    lane, q_heads = _head_queries(q_ref, tq, hd)
    r = lax.broadcasted_iota(jnp.int32, (tk, tk), 0)
    c = lax.broadcasted_iota(jnp.int32, (tk, tk), 1)
    neg_later = jnp.where(r > c, -1.0, 0.0).astype(BF16)
    neg_ones = jnp.full((tk, LANES), -1.0, BF16)
    acc_ref[...] = jnp.zeros_like(acc_ref)
    run_ref[...] = jnp.zeros_like(run_ref)

    def block(start, lo, mask):
        ks = k_ref[pl.ds(start, tk), :]
        vs = v_ref[pl.ds(start, tk), :]
        for hh in range(2):
            z = _dot_nt(q_heads[hh][lo:], ks)
            sp = jnp.maximum(z, 0.0) + jnp.log(1.0 + jnp.exp(-jnp.abs(z)))
            drop = (sp if mask is None else jnp.where(mask, sp, 0.0)).astype(BF16)
            suffix = _dot(drop, neg_later)
            total = _dot(drop, neg_ones)
            run = run_ref[hh, lo:, :]
            log_w = (z - sp) + suffix + jnp.concatenate([run] * (tk // LANES), axis=1)
            w = jnp.exp(log_w)
            if mask is not None:
                w = jnp.where(mask, w, 0.0)
            acc_ref[hh, lo:, :] += _dot(w.astype(BF16), vs)
            run_ref[hh, lo:, :] = run + total

    _sweep(i, tq, tk, block, _diag_masks(tq, tk, inclusive=False), ascending=False)
    o_ref[...] = jnp.where(lane < hd, acc_ref[0], acc_ref[1]).astype(o_ref.dtype)


def sb_attention(qkv, batch, seq, d_model, tq=512, tk=256):
    hd = d_model // ATT_HEADS
    npair = d_model // LANES
    tq = min(tq, seq)
    nq = seq // tq
    return pl.pallas_call(
        functools.partial(_sb_kernel, tq=tq, tk=tk, hd=hd),
        grid=(batch, npair, nq),
        in_specs=[pl.BlockSpec((tq, LANES), lambda b, p, i: (b * nq + i, p)),
                  pl.BlockSpec((seq, LANES), lambda b, p, i: (b, npair + p)),
                  pl.BlockSpec((seq, LANES), lambda b, p, i: (b, 2 * npair + p))],
        out_specs=pl.BlockSpec((tq, LANES), lambda b, p, i: (b * nq + i, p)),
        out_shape=jax.ShapeDtypeStruct((batch * seq, d_model), BF16),
        scratch_shapes=[pltpu.VMEM((2, tq, LANES), F32), pltpu.VMEM((2, tq, LANES), F32)],
        compiler_params=_params("parallel", "parallel", "arbitrary"),
        name="sb_attention",
    )(qkv, qkv, qkv)


def _fox_cum_kernel(f_ref, b_ref, o_ref, *, seq, blk):
    row = lax.broadcasted_iota(jnp.int32, (blk, blk), 0)
    col = lax.broadcasted_iota(jnp.int32, (blk, blk), 1)
    tri = jnp.where(col <= row, 1.0, 0.0).astype(BF16)
    run = jnp.zeros((1, f_ref.shape[-1]), F32)
    for c in range(seq // blk):
        lf = _log_sigmoid(f_ref[0, c * blk:(c + 1) * blk, :] + b_ref[...])
        hi, mid, lo = _split3(lf)
        cs = _dot(tri, hi) + _dot(tri, mid) + _dot(tri, lo) + run
        o_ref[0, c * blk:(c + 1) * blk, :] = cs
        run = cs[blk - 1:blk, :]


def fox_cum(f_pre, b_f, blk=256):
    batch, seq, nh = f_pre.shape
    blk = min(blk, seq)
    return pl.pallas_call(
        functools.partial(_fox_cum_kernel, seq=seq, blk=blk),
        grid=(batch,),
        in_specs=[pl.BlockSpec((1, seq, nh), lambda b: (b, 0, 0)),
                  pl.BlockSpec((1, nh), lambda b: (0, 0))],
        out_specs=pl.BlockSpec((1, seq, nh), lambda b: (b, 0, 0)),
        out_shape=jax.ShapeDtypeStruct((batch, seq, nh), F32),
        compiler_params=_params("parallel"),
        name="fox_cum",
    )(f_pre, b_f)


def _fox_kernel(q_ref, k_ref, v_ref, cr_ref, o_ref, acc_ref, m_ref, *, tq, tk, hd):
    i = pl.program_id(2)
    lane, q_heads = _head_queries(q_ref, tq, hd)
    klane = lax.broadcasted_iota(jnp.int32, (tk, LANES), 1)
    in_head = (klane < hd, klane >= hd)
    acc_ref[...] = jnp.zeros_like(acc_ref)
    m_ref[...] = jnp.full_like(m_ref, -jnp.inf)

    def block(start, lo, mask):
        ks = k_ref[pl.ds(start, tk), :]
        vs = v_ref[pl.ds(start, tk), :]
        for hh in range(2):
            vh = jnp.where(in_head[hh], vs, jnp.ones_like(vs))
            cum_row = cr_ref[0, 0, hh:hh + 1, pl.ds(start, tk)]
            z = _dot_nt(q_heads[hh][lo:], ks) - cum_row
            if mask is not None:
                z = jnp.where(mask, z, -jnp.inf)
            m_prev = m_ref[hh, lo:, :]
            m_new = jnp.maximum(m_prev, jnp.max(z, axis=-1, keepdims=True))
            a = jnp.exp(m_prev - m_new)
            p = jnp.exp(z - jnp.concatenate([m_new] * (tk // LANES), axis=1))
            acc_ref[hh, lo:, :] = a * acc_ref[hh, lo:, :] + _dot(p.astype(BF16), vh)
            m_ref[hh, lo:, :] = m_new

    _sweep(i, tq, tk, block, _diag_masks(tq, tk, inclusive=True), ascending=True)
    acc0 = acc_ref[0]
    acc1 = acc_ref[1]
    o = jnp.where(lane < hd, acc0 / acc0[:, hd:hd + 1], acc1 / acc1[:, 0:1])
    o_ref[...] = o.astype(o_ref.dtype)


def fox_attention(qkv, cum, batch, seq, d_model, tq=512, tk=256):
    hd = d_model // ATT_HEADS
    npair = d_model // LANES
    tq = min(tq, seq)
    nq = seq // tq
    cum_row = cum.reshape(batch, seq, npair, 2).transpose(0, 2, 3, 1)
    return pl.pallas_call(
        functools.partial(_fox_kernel, tq=tq, tk=tk, hd=hd),
        grid=(batch, npair, nq),
        in_specs=[pl.BlockSpec((tq, LANES), lambda b, p, i: (b * nq + i, p)),
                  pl.BlockSpec((seq, LANES), lambda b, p, i: (b, npair + p)),
                  pl.BlockSpec((seq, LANES), lambda b, p, i: (b, 2 * npair + p)),
                  pl.BlockSpec((1, 1, 2, seq), lambda b, p, i: (b, p, 0, 0))],
        out_specs=pl.BlockSpec((tq, LANES), lambda b, p, i: (b * nq + i, p)),
        out_shape=jax.ShapeDtypeStruct((batch * seq, d_model), BF16),
        scratch_shapes=[pltpu.VMEM((2, tq, LANES), F32), pltpu.VMEM((2, tq, LANES), F32)],
        compiler_params=_params("parallel", "parallel", "arbitrary"),
        name="fox_attention",
    )(qkv, qkv, qkv, cum_row)


def _mlstm_kernel(q_ref, k_ref, v_ref, og_ref, gc_ref, gr_ref, o_ref, c_ref, n_ref, m_ref,
                  *, nh, dk, dv, L):
    @pl.when(pl.program_id(1) == 0)
    def _():
        c_ref[...] = jnp.zeros_like(c_ref)
        n_ref[...] = jnp.zeros_like(n_ref)
        m_ref[...] = jnp.zeros_like(m_ref)

    row = lax.broadcasted_iota(jnp.int32, (L, L), 0)
    col = lax.broadcasted_iota(jnp.int32, (L, L), 1)
    causal = col <= row
    gcol = gc_ref[0]
    grow = gr_ref[0, 0]
    scale = dk ** -0.5
    for h in range(nh):
        i_col = gcol[:, h:h + 1]
        i_row = grow[h:h + 1, :]
        lf_col = _log_sigmoid(gcol[:, nh + h:nh + h + 1])
        lf_row = _log_sigmoid(grow[nh + h:nh + h + 1, :])
        bcum_col = jnp.sum(jnp.where(causal, lf_row, 0.0), axis=1, keepdims=True)
        bcum_row = jnp.sum(jnp.where(row <= col, lf_col, 0.0), axis=0, keepdims=True)
        m_prev = m_ref[h]
        d = jnp.where(causal, bcum_col - bcum_row + i_row, -jnp.inf)
        inter = bcum_col + m_prev
        m_t = jnp.maximum(inter, jnp.max(d, axis=1, keepdims=True))
        w_intra = jnp.exp(d - m_t)
        w_inter = jnp.exp(inter - m_t)
        qh = q_ref[:, h * dk:(h + 1) * dk]
        kh = k_ref[:, h * dk:(h + 1) * dk]
        vh = v_ref[:, h * dv:(h + 1) * dv]
        qk = _dot_nt(qh, kh) * (scale * w_intra)
        n_row = n_ref[h]
        num = w_inter * _dot(qh, c_ref[h].astype(BF16)) + _dot(qk.astype(BF16), vh)
        den = (w_inter * jnp.sum(qh.astype(F32) * n_row, axis=1, keepdims=True)
               + jnp.sum(qk, axis=1, keepdims=True))
        h_out = num / jnp.maximum(jnp.abs(den), jnp.exp(-m_t))
        m_new = m_t[L - 1:L, :]
        blast = bcum_col[L - 1:L, :]
        decay = jnp.exp(blast + m_prev - m_new)
        kw = (kh.astype(F32) * scale) * jnp.exp(blast - bcum_col + i_col - m_new)
        c_ref[h] = decay * c_ref[h] + _dot_tn(kw.astype(BF16), vh)
        n_ref[h] = decay * n_row + jnp.sum(kw, axis=0, keepdims=True)
        m_ref[h] = m_new
        gate = _sigmoid(og_ref[:, h * dv:(h + 1) * dv].astype(F32))
        o_ref[:, h * dv:(h + 1) * dv] = (h_out * gate).astype(o_ref.dtype)


def mlstm_core(proj, gates, batch, seq, d_model):
    nh, L = ML_HEADS, CHUNK
    dk = d_model // 2 // nh
    dv = d_model // nh
    nc = seq // L
    qw = nh * dk
    gates_row = gates.reshape(batch, nc, L, 2 * nh).transpose(0, 1, 3, 2)
    return pl.pallas_call(
        functools.partial(_mlstm_kernel, nh=nh, dk=dk, dv=dv, L=L),
        grid=(batch, nc),
        in_specs=[pl.BlockSpec((L, qw), lambda b, c: (b * nc + c, 0)),
                  pl.BlockSpec((L, qw), lambda b, c: (b * nc + c, 1)),
                  pl.BlockSpec((L, d_model), lambda b, c: (b * nc + c, 1)),
                  pl.BlockSpec((L, d_model), lambda b, c: (b * nc + c, 2)),
                  pl.BlockSpec((1, L, 2 * nh), lambda b, c: (b, c, 0)),
                  pl.BlockSpec((1, 1, 2 * nh, L), lambda b, c: (b, c, 0, 0))],
        out_specs=pl.BlockSpec((L, d_model), lambda b, c: (b * nc + c, 0)),
        out_shape=jax.ShapeDtypeStruct((batch * seq, d_model), BF16),
        scratch_shapes=[pltpu.VMEM((nh, dk, dv), F32),
                        pltpu.VMEM((nh, 1, dk), F32),
                        pltpu.VMEM((nh, 1, 1), F32)],
        compiler_params=_params("parallel", "arbitrary"),
        name="mlstm_core",
    )(proj, proj, proj, proj, gates, gates_row)


def _gla_kernel(q_ref, k_ref, v_ref, r_ref, a_ref, wh_ref, wl_ref, ba_ref, ng_ref, o_ref, st_ref,
                *, nh, dk, dv, L):
    @pl.when(pl.program_id(1) == 0)
    def _():
        st_ref[...] = jnp.zeros_like(st_ref)

    row = lax.broadcasted_iota(jnp.int32, (L, L), 0)
    col = lax.broadcasted_iota(jnp.int32, (L, L), 1)
    causal = col <= row
    tri = jnp.where(causal, 1.0, 0.0).astype(BF16)
    a_hi, a_lo = _split2(a_ref[...])
    scale = dk ** -0.5
    for h in range(nh):
        hs = slice(h * dk, (h + 1) * dk)
        wh = wh_ref[:, hs]
        logit = _dot(a_hi, wh) + _dot(a_lo, wh) + _dot(a_hi, wl_ref[:, hs]) + ba_ref[:, hs]
        la = _log_sigmoid(logit) * (1.0 / GLA_TAU)
        la_hi, la_lo = _split2(la)
        bcum = _dot(tri, la_hi) + _dot(tri, la_lo)
        blast = bcum[L - 1:L, :]
        ref = bcum[L // 2 - 1:L // 2, :]
        qf = q_ref[:, hs].astype(F32) * scale
        kf = k_ref[:, hs].astype(F32)
        vh = v_ref[:, h * dv:(h + 1) * dv]
        q_rel = (qf * jnp.exp(bcum - ref)).astype(BF16)
        k_rel = (kf * jnp.exp(ref - bcum)).astype(BF16)
        q_dec = (qf * jnp.exp(bcum)).astype(BF16)
        k_dec = (kf * jnp.exp(blast - bcum)).astype(BF16)
        att = jnp.where(causal, _dot_nt(q_rel, k_rel), 0.0)
        st = st_ref[h]
        o = _dot(att.astype(BF16), vh) + _dot_nt(q_dec, st.astype(BF16))
        st_ref[h] = st * jnp.exp(blast) + _dot_tn(vh, k_dec)
        o = o * lax.rsqrt(jnp.mean(o * o, axis=-1, keepdims=True) + RMS_EPS) * ng_ref[...]
        r = r_ref[:, h * dv:(h + 1) * dv].astype(F32)
        o_ref[:, h * dv:(h + 1) * dv] = (o * (r * _sigmoid(r))).astype(o_ref.dtype)


def gla_core(proj, a_low, wa_hi, wa_lo, b_a, norm_g, batch, seq, d_model):
    nh, L = GLA_HEADS, CHUNK
    dk = d_model // 2 // nh
    dv = d_model // nh
    nc = seq // L
    qw = nh * dk
    return pl.pallas_call(
        functools.partial(_gla_kernel, nh=nh, dk=dk, dv=dv, L=L),
        grid=(batch, nc),
        in_specs=[pl.BlockSpec((L, qw), lambda b, c: (b * nc + c, 0)),
                  pl.BlockSpec((L, qw), lambda b, c: (b * nc + c, 1)),
                  pl.BlockSpec((L, d_model), lambda b, c: (b * nc + c, 1)),
                  pl.BlockSpec((L, d_model), lambda b, c: (b * nc + c, 2)),
                  pl.BlockSpec((L, LANES), lambda b, c: (b * nc + c, 0)),
                  pl.BlockSpec((LANES, qw), lambda b, c: (0, 0)),
                  pl.BlockSpec((LANES, qw), lambda b, c: (0, 0)),
                  pl.BlockSpec((1, qw), lambda b, c: (0, 0)),
                  pl.BlockSpec((1, dv), lambda b, c: (0, 0))],
        out_specs=pl.BlockSpec((L, d_model), lambda b, c: (b * nc + c, 0)),
        out_shape=jax.ShapeDtypeStruct((batch * seq, d_model), BF16),
        scratch_shapes=[pltpu.VMEM((nh, dv, dk), F32)],
        compiler_params=_params("parallel", "arbitrary"),
        name="gla_core",
    )(proj, proj, proj, proj, a_low, wa_hi, wa_lo, b_a, norm_g)


def _xattn_kernel(hb_ref, hf_ref, wq_ref, kv_ref, wo_ref, g_ref, b_ref, of_ref, ob_ref,
                  *, nh, alpha):
    d = wq_ref.shape[1]
    hd = d // nh
    q = (_dot(hb_ref[...], wq_ref[...]) * (hd ** -0.5)).astype(BF16)
    outs = []
    for h in range(nh):
        kh = kv_ref[:, h * hd:(h + 1) * hd]
        vh = kv_ref[:, d + h * hd:d + (h + 1) * hd]
        s = _dot_nt(q[:, h * hd:(h + 1) * hd], kh)
        p = jnp.exp(s - jnp.max(s, axis=-1, keepdims=True))
        l = jnp.sum(p, axis=-1, keepdims=True)
        outs.append((_dot(p.astype(BF16), vh) / l).astype(BF16))
    o = jnp.concatenate(outs, axis=-1)
    z = alpha * hf_ref[...] + _dot(o, wo_ref[...])
    y = _layer_norm(z, g_ref[...], b_ref[...])
    of_ref[...] = y
    ob_ref[...] = y.astype(BF16)


def xattn_block(hb, hf, wq, kv, wo, g, b, batch, seq, n_mem, alpha, tm):
    d = hb.shape[1]
    tm = min(tm, seq)
    nt = seq // tm
    return pl.pallas_call(
        functools.partial(_xattn_kernel, nh=XA_HEADS, alpha=alpha),
        grid=(batch, nt),
        in_specs=[pl.BlockSpec((tm, d), lambda bi, i: (bi * nt + i, 0)),
                  pl.BlockSpec((tm, d), lambda bi, i: (bi * nt + i, 0)),
                  pl.BlockSpec((d, d), lambda bi, i: (0, 0)),
                  pl.BlockSpec((n_mem, 2 * d), lambda bi, i: (bi, 0)),
                  pl.BlockSpec((d, d), lambda bi, i: (0, 0)),
                  pl.BlockSpec((1, d), lambda bi, i: (0, 0)),
                  pl.BlockSpec((1, d), lambda bi, i: (0, 0))],
        out_specs=[pl.BlockSpec((tm, d), lambda bi, i: (bi * nt + i, 0)),
                   pl.BlockSpec((tm, d), lambda bi, i: (bi * nt + i, 0))],
        out_shape=[jax.ShapeDtypeStruct((batch * seq, d), F32),
                   jax.ShapeDtypeStruct((batch * seq, d), BF16)],
        compiler_params=_params("parallel", "parallel"),
        name="xattn_block",
    )(hb, hf, wq, kv, wo, g, b)


def _gelu_tanh(x):
    return 0.5 * x * (1.0 + jnp.tanh(0.7978845608028654 * (x + 0.044715 * (x * x * x))))


def _ffn_kernel(hb_ref, hp_ref, hf_ref, wg_ref, wv_ref, cg_ref, cv_ref, bg_ref, bv_ref, wd_ref,
                g_ref, b_ref, of_ref, ob_ref, acc_ref, *, tm, blocks_per_seq, alpha):
    i = pl.program_id(0)
    j = pl.program_id(1)

    @pl.when(j == 0)
    def _():
        acc_ref[...] = jnp.zeros_like(acc_ref)

    hb = hb_ref[...]
    hp = hp_ref[...]
    hp = jnp.where(i % blocks_per_seq == 0, jnp.zeros_like(hp), hp)
    halo = hp.shape[0]
    tf = wg_ref.shape[1]
    row = lax.broadcasted_iota(jnp.int32, (tm, tf), 0)

    def branch(w_ref, c_ref, bias_ref):
        w = w_ref[...]
        u = _dot(hb, w)
        up = _dot(hp, w)
        p1 = up[halo - 1:halo, :]
        p2 = up[halo - 2:halo - 1, :]
        u1 = jnp.where(row == 0, p1, pltpu.roll(u, 1, 0))
        u2 = jnp.where(row == 0, p2, jnp.where(row == 1, p1, pltpu.roll(u, 2, 0)))
        c = c_ref[...]
        return c[0:1, :] * u2 + c[1:2, :] * u1 + c[2:3, :] * u + bias_ref[...]

    gate = branch(wg_ref, cg_ref, bg_ref)
    val = branch(wv_ref, cv_ref, bv_ref)
    act = (_gelu_tanh(gate) * val).astype(BF16)
    acc_ref[...] += _dot(act, wd_ref[...])

    @pl.when(j == pl.num_programs(1) - 1)
    def _():
        y = _layer_norm(alpha * hf_ref[...] + acc_ref[...], g_ref[...], b_ref[...])
        of_ref[...] = y
        ob_ref[...] = y.astype(BF16)


def ffn_block(hb, hf, w_up, conv_w, conv_b, w_down, g, b, seq, alpha, tm, tf):
    t, d = hb.shape
    f = w_down.shape[0]
    tm = min(tm, seq)
    nf = f // tf
    halo = BF16_SUBLANES
    hpb = tm // halo
    return pl.pallas_call(
        functools.partial(_ffn_kernel, tm=tm, blocks_per_seq=seq // tm, alpha=alpha),
        grid=(t // tm, nf),
        in_specs=[pl.BlockSpec((tm, d), lambda i, j: (i, 0)),
                  pl.BlockSpec((halo, d), lambda i, j: (jnp.maximum(i * hpb - 1, 0), 0)),
                  pl.BlockSpec((tm, d), lambda i, j: (i, 0)),
                  pl.BlockSpec((d, tf), lambda i, j: (0, j)),
                  pl.BlockSpec((d, tf), lambda i, j: (0, nf + j)),
                  pl.BlockSpec((CONV_W, tf), lambda i, j: (0, j)),
                  pl.BlockSpec((CONV_W, tf), lambda i, j: (0, nf + j)),
                  pl.BlockSpec((1, tf), lambda i, j: (0, j)),
                  pl.BlockSpec((1, tf), lambda i, j: (0, nf + j)),
                  pl.BlockSpec((tf, d), lambda i, j: (j, 0)),
                  pl.BlockSpec((1, d), lambda i, j: (0, 0)),
                  pl.BlockSpec((1, d), lambda i, j: (0, 0))],
        out_specs=[pl.BlockSpec((tm, d), lambda i, j: (i, 0)),
                   pl.BlockSpec((tm, d), lambda i, j: (i, 0))],
        out_shape=[jax.ShapeDtypeStruct((t, d), F32),
                   jax.ShapeDtypeStruct((t, d), BF16)],
        scratch_shapes=[pltpu.VMEM((tm, d), F32)],
        compiler_params=_params("parallel", "arbitrary"),
        name="ffn_block",
    )(hb, hb, hf, w_up, w_up, conv_w, conv_w, conv_b, conv_b, w_down, g, b)


def _pad_cols(w, width):
    return jnp.pad(w, ((0, 0), (0, width - w.shape[1])))


def kernel(x, mem, ln_g, ln_b, mix_wo, sb_win, fox_win, fox_bf, ml_win, ml_bi, ml_bf,
           gla_win, gla_wa2, gla_ba, gla_norm_g, xa_wq, xa_wkv, xa_wo,
           ffn_up, ffn_conv, ffn_conv_b, ffn_down):
    batch, seq, d = x.shape
    n_mem = mem.shape[1]
    depth = mix_wo.shape[0]
    t = batch * seq
    alpha = (2.0 * depth) ** 0.25
    main = 3 * d

    hf = x.reshape(t, d)
    hb = hf.astype(BF16)
    memb = mem.reshape(batch * n_mem, d).astype(BF16)

    for layer in range(depth):
        kind = layer % 4
        occ = layer // 4
        w_in = (sb_win, fox_win, ml_win, gla_win)[kind][occ]
        proj = linear(hb, w_in[:, :main].astype(BF16), BF16, tm=1024, tn=1024)
        if kind == 0:
            y = sb_attention(proj, batch, seq, d)
        elif kind == 1:
            f_pre = linear(hb, _pad_cols(w_in[:, main:], LANES).astype(BF16), F32, tm=1024, tn=LANES)
            cum = fox_cum(f_pre.reshape(batch, seq, LANES), _pad_cols(fox_bf[occ][None, :], LANES))
            y = fox_attention(proj, cum[:, :, :ATT_HEADS], batch, seq, d)
        elif kind == 2:
            g_pre = linear(hb, _pad_cols(w_in[:, main:], LANES).astype(BF16), F32, tm=1024, tn=LANES)
            gates = g_pre[:, :2 * ML_HEADS] + jnp.concatenate([ml_bi[occ], ml_bf[occ]])[None, :]
            y = mlstm_core(proj, gates.reshape(batch, seq, 2 * ML_HEADS), batch, seq, d)
        else:
            a_low = linear(hb, _pad_cols(w_in[:, main:], LANES).astype(BF16), F32, tm=1024, tn=LANES)
            wa = jnp.pad(gla_wa2[occ], ((0, LANES - GLA_RANK), (0, 0)))
            wa_hi = wa.astype(BF16)
            wa_lo = (wa - wa_hi.astype(F32)).astype(BF16)
            y = gla_core(proj, a_low, wa_hi, wa_lo, gla_ba[occ][None, :], gla_norm_g[occ][None, :],
                         batch, seq, d)
        hf, hb = linear_res_ln(y, mix_wo[layer].astype(BF16), hf, ln_g[layer, 0][None, :],
                               ln_b[layer, 0][None, :], alpha, tm=512)

        kv = linear(memb, xa_wkv[layer].astype(BF16), BF16, tm=1024, tn=1024)
        hf, hb = xattn_block(hb, hf, xa_wq[layer].astype(BF16), kv, xa_wo[layer].astype(BF16),
                             ln_g[layer, 1][None, :], ln_b[layer, 1][None, :],
                             batch, seq, n_mem, alpha, tm=512)

        hf, hb = ffn_block(hb, hf, ffn_up[layer].astype(BF16), ffn_conv[layer, :, 0, :],
                           ffn_conv_b[layer][None, :], ffn_down[layer].astype(BF16),
                           ln_g[layer, 2][None, :], ln_b[layer, 2][None, :],
                           seq, alpha, tm=512, tf=256)
    return hf.reshape(batch, seq, d)
```

```python
import functools

import jax
import jax.numpy as jnp
from jax import lax
from jax.experimental import pallas as pl
from jax.experimental.pallas import tpu as pltpu

F32 = jnp.float32
BF16 = jnp.bfloat16

ATT_HEADS = 16
ML_HEADS = 4
GLA_HEADS = 4
GLA_RANK = 16
GLA_TAU = 16.0
XA_HEADS = 4
CHUNK = 64
CONV_W = 3
LN_EPS = 1e-5
RMS_EPS = 1e-6

LANES = 128
SUBLANES = 8
VMEM_LIMIT = 48 * 1024 * 1024

_NT = (((1,), (1,)), ((), ()))
_TN = (((0,), (0,)), ((), ()))


def _params(*sem):
    return pltpu.CompilerParams(dimension_semantics=sem, vmem_limit_bytes=VMEM_LIMIT)


def _dot(a, b):
    return jnp.dot(a, b, preferred_element_type=F32)


def _dot_nt(a, b):
    return lax.dot_general(a, b, _NT, preferred_element_type=F32)


def _dot_tn(a, b):
    return lax.dot_general(a, b, _TN, preferred_element_type=F32)


def _log_sigmoid(x):
    return jnp.minimum(x, 0.0) - jnp.log1p(jnp.exp(-jnp.abs(x)))


def _sigmoid(x):
    return 1.0 / (1.0 + jnp.exp(-x))


def _split2(x):
    hi = x.astype(BF16)
    lo = (x - hi.astype(F32)).astype(BF16)
    return hi, lo


def _split3(x):
    hi = x.astype(BF16)
    r = x - hi.astype(F32)
    mid = r.astype(BF16)
    lo = (r - mid.astype(F32)).astype(BF16)
    return hi, mid, lo


def _layer_norm(z, g, b):
    mu = jnp.mean(z, axis=-1, keepdims=True)
    zc = z - mu
    var = jnp.mean(zc * zc, axis=-1, keepdims=True)
    return zc * lax.rsqrt(var + LN_EPS) * g + b


def _linear_kernel(x_ref, w_ref, o_ref):
    o_ref[...] = _dot(x_ref[...], w_ref[...]).astype(o_ref.dtype)


def linear(x, w, out_dtype, tm, tn):
    m, k = x.shape
    n = w.shape[1]
    tm = min(tm, m)
    tn = min(tn, n)
    return pl.pallas_call(
        _linear_kernel,
        grid=(m // tm, n // tn),
        in_specs=[pl.BlockSpec((tm, k), lambda i, j: (i, 0)),
                  pl.BlockSpec((k, tn), lambda i, j: (0, j))],
        out_specs=pl.BlockSpec((tm, tn), lambda i, j: (i, j)),
        out_shape=jax.ShapeDtypeStruct((m, n), out_dtype),
        compiler_params=_params("parallel", "arbitrary"),
        name="linear",
    )(x, w)


def _linear_res_ln_kernel(y_ref, w_ref, h_ref, g_ref, b_ref, of_ref, ob_ref, *, alpha):
    z = alpha * h_ref[...] + _dot(y_ref[...], w_ref[...])
    o = _layer_norm(z, g_ref[...], b_ref[...])
    of_ref[...] = o
    ob_ref[...] = o.astype(BF16)


def linear_res_ln(y, w, h, g, b, alpha, tm):
    m, k = y.shape
    d = w.shape[1]
    tm = min(tm, m)
    return pl.pallas_call(
        functools.partial(_linear_res_ln_kernel, alpha=alpha),
        grid=(m // tm,),
        in_specs=[pl.BlockSpec((tm, k), lambda i: (i, 0)),
                  pl.BlockSpec((k, d), lambda i: (0, 0)),
                  pl.BlockSpec((tm, d), lambda i: (i, 0)),
                  pl.BlockSpec((1, d), lambda i: (0, 0)),
                  pl.BlockSpec((1, d), lambda i: (0, 0))],
        out_specs=[pl.BlockSpec((tm, d), lambda i: (i, 0)),
                   pl.BlockSpec((tm, d), lambda i: (i, 0))],
        out_shape=[jax.ShapeDtypeStruct((m, d), F32),
                   jax.ShapeDtypeStruct((m, d), BF16)],
        compiler_params=_params("parallel"),
        name="linear_res_ln",
    )(y, w, h, g, b)


def _head_queries(q_ref, tq, hd):
    lane = lax.broadcasted_iota(jnp.int32, (tq, LANES), 1)
    qs = q_ref[...] * (hd ** -0.5)
    zero = jnp.zeros_like(qs)
    return lane, (jnp.where(lane < hd, qs, zero), jnp.where(lane >= hd, qs, zero))


def _sweep(i, tq, tk, block, masks, ascending):
    nd = tq // tk
    nfull = i * nd

    def diagonal():
        for c in (range(nd) if ascending else reversed(range(nd))):
            block(pl.multiple_of(i * tq + c * tk, tk), c * tk, masks[c])

    def full_body(j, carry):
        kb = j if ascending else nfull - 1 - j
        block(pl.multiple_of(kb * tk, tk), 0, None)
        return carry

    if ascending:
        lax.fori_loop(0, nfull, full_body, 0)
        diagonal()
    else:
        diagonal()
        lax.fori_loop(0, nfull, full_body, 0)


def _diag_masks(tq, tk, inclusive):
    masks = []
    for c in range(tq // tk):
        n = tq - c * tk
        col = lax.broadcasted_iota(jnp.int32, (n, tk), 1)
        row = lax.broadcasted_iota(jnp.int32, (n, tk), 0)
        masks.append(col <= row if inclusive else col < row)
    return masks


def _sb_kernel(q_ref, k_ref, v_ref, o_ref, acc_ref, run_ref, *, tq, tk, hd):
    i = pl.program_id(2)
    lane, q_heads = _head_queries(q_ref, tq, hd)
    r = lax.broadcasted_iota(jnp.int32, (tk, tk), 0)
    c = lax.broadcasted_iota(jnp.int32, (tk, tk), 1)
    neg_later = jnp.where(r > c, -1.0, 0.0).astype(BF16)
    neg_ones = jnp.full((tk, LANES), -1.0, BF16)
    acc_ref[...] = jnp.zeros_like(acc_ref)
    run_ref[...] = jnp.zeros_like(run_ref)

    def block(start, lo, mask):
        ks = k_ref[pl.ds(start, tk), :]
        vs = v_ref[pl.ds(start, tk), :]
        for hh in range(2):
            z = _dot_nt(q_heads[hh][lo:], ks)
            sp = jnp.maximum(z, 0.0) + jnp.log(1.0 + jnp.exp(-jnp.abs(z)))
            drop = (sp if mask is None else jnp.where(mask, sp, 0.0)).astype(BF16)
            suffix = _dot(drop, neg_later)
            total = _dot(drop, neg_ones)
            run = run_ref[hh, lo:, :]
            log_w = (z - sp) + suffix + jnp.concatenate([run] * (tk // LANES), axis=1)
            w = jnp.exp(log_w)
            if mask is not None:
                w = jnp.where(mask, w, 0.0)
            acc_ref[hh, lo:, :] += _dot(w.astype(BF16), vs)
            run_ref[hh, lo:, :] = run + total

    _sweep(i, tq, tk, block, _diag_masks(tq, tk, inclusive=False), ascending=False)
    o_ref[...] = jnp.where(lane < hd, acc_ref[0], acc_ref[1]).astype(o_ref.dtype)


def sb_attention(qkv, batch, seq, d_model, tq=512, tk=256):
    hd = d_model // ATT_HEADS
    npair = d_model // LANES
    tq = min(tq, seq)
    nq = seq // tq
    return pl.pallas_call(
        functools.partial(_sb_kernel, tq=tq, tk=tk, hd=hd),
        grid=(batch, npair, nq),
        in_specs=[pl.BlockSpec((tq, LANES), lambda b, p, i: (b * nq + i, p)),
                  pl.BlockSpec((seq, LANES), lambda b, p, i: (b, npair + p)),
                  pl.BlockSpec((seq, LANES), lambda b, p, i: (b, 2 * npair + p))],
        out_specs=pl.BlockSpec((tq, LANES), lambda b, p, i: (b * nq + i, p)),
        out_shape=jax.ShapeDtypeStruct((batch * seq, d_model), BF16),
        scratch_shapes=[pltpu.VMEM((2, tq, LANES), F32), pltpu.VMEM((2, tq, LANES), F32)],
        compiler_params=_params("parallel", "parallel", "arbitrary"),
        name="sb_attention",
    )(qkv, qkv, qkv)


def _fox_cum_kernel(f_ref, b_ref, o_ref, *, seq, blk):
    row = lax.broadcasted_iota(jnp.int32, (blk, blk), 0)
    col = lax.broadcasted_iota(jnp.int32, (blk, blk), 1)
    tri = jnp.where(col <= row, 1.0, 0.0).astype(BF16)
    run = jnp.zeros((1, f_ref.shape[-1]), F32)
    for c in range(seq // blk):
        lf = _log_sigmoid(f_ref[0, c * blk:(c + 1) * blk, :] + b_ref[...])
        hi, mid, lo = _split3(lf)
        cs = _dot(tri, hi) + _dot(tri, mid) + _dot(tri, lo) + run
        o_ref[0, c * blk:(c + 1) * blk, :] = cs
        run = cs[blk - 1:blk, :]


def fox_cum(f_pre, b_f, blk=256):
    batch, seq, nh = f_pre.shape
    blk = min(blk, seq)
    return pl.pallas_call(
        functools.partial(_fox_cum_kernel, seq=seq, blk=blk),
        grid=(batch,),
        in_specs=[pl.BlockSpec((1, seq, nh), lambda b: (b, 0, 0)),
                  pl.BlockSpec((1, nh), lambda b: (0, 0))],
        out_specs=pl.BlockSpec((1, seq, nh), lambda b: (b, 0, 0)),
        out_shape=jax.ShapeDtypeStruct((batch, seq, nh), F32),
        compiler_params=_params("parallel"),
        name="fox_cum",
    )(f_pre, b_f)


def _fox_kernel(q_ref, k_ref, v_ref, cr_ref, o_ref, acc_ref, m_ref, *, tq, tk, hd):
    i = pl.program_id(2)
    lane, q_heads = _head_queries(q_ref, tq, hd)
    klane = lax.broadcasted_iota(jnp.int32, (tk, LANES), 1)
    in_head = (klane < hd, klane >= hd)
    acc_ref[...] = jnp.zeros_like(acc_ref)
    m_ref[...] = jnp.full_like(m_ref, -jnp.inf)

    def block(start, lo, mask):
        ks = k_ref[pl.ds(start, tk), :]
        vs = v_ref[pl.ds(start, tk), :]
        heads = range(2)
        vh = [jnp.where(in_head[hh], vs, jnp.ones_like(vs)) for hh in heads]
        s = [_dot_nt(q_heads[hh][lo:], ks) for hh in heads]
        p, a = [], []
        for hh in heads:
            cum_row = cr_ref[0, 0, hh:hh + 1, pl.ds(start, tk)]
            z = s[hh] - cum_row
            if mask is not None:
                z = jnp.where(mask, z, -jnp.inf)
            m_prev = m_ref[hh, lo:, :]
            m_new = jnp.maximum(m_prev, jnp.max(z, axis=-1, keepdims=True))
            a.append(jnp.exp(m_prev - m_new))
            p.append(jnp.exp(z - jnp.concatenate([m_new] * (tk // LANES), axis=1)).astype(BF16))
            m_ref[hh, lo:, :] = m_new
        pv = [_dot(p[hh], vh[hh]) for hh in heads]
        for hh in heads:
            acc_ref[hh, lo:, :] = a[hh] * acc_ref[hh, lo:, :] + pv[hh]

    _sweep(i, tq, tk, block, _diag_masks(tq, tk, inclusive=True), ascending=True)
    acc0 = acc_ref[0]
    acc1 = acc_ref[1]
    o = jnp.where(lane < hd, acc0 / acc0[:, hd:hd + 1], acc1 / acc1[:, 0:1])
    o_ref[...] = o.astype(o_ref.dtype)


def fox_attention(qkv, cum, batch, seq, d_model, tq=512, tk=256):
    hd = d_model // ATT_HEADS
    npair = d_model // LANES
    tq = min(tq, seq)
    nq = seq // tq
    cum_row = cum.reshape(batch, seq, npair, 2).transpose(0, 2, 3, 1)
    return pl.pallas_call(
        functools.partial(_fox_kernel, tq=tq, tk=tk, hd=hd),
        grid=(batch, npair, nq),
        in_specs=[pl.BlockSpec((tq, LANES), lambda b, p, i: (b * nq + i, p)),
                  pl.BlockSpec((seq, LANES), lambda b, p, i: (b, npair + p)),
                  pl.BlockSpec((seq, LANES), lambda b, p, i: (b, 2 * npair + p)),
                  pl.BlockSpec((1, 1, 2, seq), lambda b, p, i: (b, p, 0, 0))],
        out_specs=pl.BlockSpec((tq, LANES), lambda b, p, i: (b * nq + i, p)),
        out_shape=jax.ShapeDtypeStruct((batch * seq, d_model), BF16),
        scratch_shapes=[pltpu.VMEM((2, tq, LANES), F32), pltpu.VMEM((2, tq, LANES), F32)],
        compiler_params=_params("parallel", "parallel", "arbitrary"),
        name="fox_attention",
    )(qkv, qkv, qkv, cum_row)


def _mlstm_kernel(q_ref, k_ref, v_ref, og_ref, gc_ref, gr_ref, o_ref, c_ref, n_ref, m_ref,
                  *, nh, dk, dv, L):
    @pl.when(pl.program_id(1) == 0)
    def _():
        c_ref[...] = jnp.zeros_like(c_ref)
        n_ref[...] = jnp.zeros_like(n_ref)
        m_ref[...] = jnp.zeros_like(m_ref)

    row = lax.broadcasted_iota(jnp.int32, (L, L), 0)
    col = lax.broadcasted_iota(jnp.int32, (L, L), 1)
    causal = col <= row
    scale = dk ** -0.5
    bb = q_ref.shape[0]
    chains = [(bi, h) for bi in range(bb) for h in range(nh)]
    q = [q_ref[bi, :, h * dk:(h + 1) * dk] for bi, h in chains]
    k = [k_ref[bi, :, h * dk:(h + 1) * dk] for bi, h in chains]
    v = [v_ref[bi, :, h * dv:(h + 1) * dv] for bi, h in chains]

    qk_raw = [_dot_nt(q[s], k[s]) for s in range(len(chains))]
    q_state = [_dot(q[s], c_ref[s].astype(BF16)) for s in range(len(chains))]

    w_intra, w_inter, m_t, m_new, decay, kw = [], [], [], [], [], []
    for s, (bi, h) in enumerate(chains):
        gcol = gc_ref[bi]
        grow = gr_ref[bi, 0]
        i_col = gcol[:, h:h + 1]
        i_row = grow[h:h + 1, :]
        lf_col = _log_sigmoid(gcol[:, nh + h:nh + h + 1])
        lf_row = _log_sigmoid(grow[nh + h:nh + h + 1, :])
        bcum_col = jnp.sum(jnp.where(causal, lf_row, 0.0), axis=1, keepdims=True)
        bcum_row = jnp.sum(jnp.where(row <= col, lf_col, 0.0), axis=0, keepdims=True)
        m_prev = m_ref[s]
        d = jnp.where(causal, bcum_col - bcum_row + i_row, -jnp.inf)
        inter = bcum_col + m_prev
        mt = jnp.maximum(inter, jnp.max(d, axis=1, keepdims=True))
        mn = mt[L - 1:L, :]
        blast = bcum_col[L - 1:L, :]
        m_t.append(mt)
        m_new.append(mn)
        w_intra.append(jnp.exp(d - mt))
        w_inter.append(jnp.exp(inter - mt))
        decay.append(jnp.exp(blast + m_prev - mn))
        kw.append((k[s].astype(F32) * scale) * jnp.exp(blast - bcum_col + i_col - mn))

    qk = [qk_raw[s] * (scale * w_intra[s]) for s in range(len(chains))]
    intra = [_dot(qk[s].astype(BF16), v[s]) for s in range(len(chains))]
    kv = [_dot_tn(kw[s].astype(BF16), v[s]) for s in range(len(chains))]

    new_c, new_n, outs = [], [], []
    for s, (bi, h) in enumerate(chains):
        n_row = n_ref[s]
        num = w_inter[s] * q_state[s] + intra[s]
        den = (w_inter[s] * jnp.sum(q[s].astype(F32) * n_row, axis=1, keepdims=True)
               + jnp.sum(qk[s], axis=1, keepdims=True))
        h_out = num / jnp.maximum(jnp.abs(den), jnp.exp(-m_t[s]))
        new_c.append(decay[s] * c_ref[s] + kv[s])
        new_n.append(decay[s] * n_row + jnp.sum(kw[s], axis=0, keepdims=True))
        gate = _sigmoid(og_ref[bi, :, h * dv:(h + 1) * dv].astype(F32))
        outs.append((h_out * gate).astype(o_ref.dtype))
    c_ref[...] = jnp.stack(new_c)
    n_ref[...] = jnp.stack(new_n)
    m_ref[...] = jnp.stack(m_new)
    o_ref[...] = jnp.stack([jnp.concatenate(outs[bi * nh:(bi + 1) * nh], axis=1) for bi in range(bb)])


def mlstm_core(proj, gates, batch, seq, d_model, bb):
    nh, L = ML_HEADS, CHUNK
    dk = d_model // 2 // nh
    dv = d_model // nh
    nc = seq // L
    qw = nh * dk
    bb = min(bb, batch)
    proj = proj.reshape(batch, seq, 3 * d_model)
    gates_row = gates.reshape(batch, nc, L, 2 * nh).transpose(0, 1, 3, 2)
    out = pl.pallas_call(
        functools.partial(_mlstm_kernel, nh=nh, dk=dk, dv=dv, L=L),
        grid=(batch // bb, nc),
        in_specs=[pl.BlockSpec((bb, L, qw), lambda g, c: (g, c, 0)),
                  pl.BlockSpec((bb, L, qw), lambda g, c: (g, c, 1)),
                  pl.BlockSpec((bb, L, d_model), lambda g, c: (g, c, 1)),
                  pl.BlockSpec((bb, L, d_model), lambda g, c: (g, c, 2)),
                  pl.BlockSpec((bb, L, 2 * nh), lambda g, c: (g, c, 0)),
                  pl.BlockSpec((bb, 1, 2 * nh, L), lambda g, c: (g, c, 0, 0))],
        out_specs=pl.BlockSpec((bb, L, d_model), lambda g, c: (g, c, 0)),
        out_shape=jax.ShapeDtypeStruct((batch, seq, d_model), BF16),
        scratch_shapes=[pltpu.VMEM((bb * nh, dk, dv), F32),
                        pltpu.VMEM((bb * nh, 1, dk), F32),
                        pltpu.VMEM((bb * nh, 1, 1), F32)],
        compiler_params=_params("parallel", "arbitrary"),
        name="mlstm_core",
    )(proj, proj, proj, proj, gates, gates_row)
    return out.reshape(batch * seq, d_model)


def _gla_kernel(q_ref, k_ref, v_ref, r_ref, a_ref, wh_ref, wl_ref, ba_ref, ng_ref, o_ref, st_ref,
                *, nh, dk, dv, L):
    @pl.when(pl.program_id(1) == 0)
    def _():
        st_ref[...] = jnp.zeros_like(st_ref)

    row = lax.broadcasted_iota(jnp.int32, (L, L), 0)
    col = lax.broadcasted_iota(jnp.int32, (L, L), 1)
    causal = col <= row
    tri = jnp.where(causal, 1.0, 0.0).astype(BF16)
    scale = dk ** -0.5
    bb = q_ref.shape[0]
    chains = [(bi, h) for bi in range(bb) for h in range(nh)]

    logits = []
    for bi in range(bb):
        a_hi, a_lo = _split2(a_ref[bi])
        wh = wh_ref[...]
        logits.append(_dot(a_hi, wh) + _dot(a_lo, wh) + _dot(a_hi, wl_ref[...]) + ba_ref[...])

    bcums = []
    for bi in range(bb):
        la_hi, la_lo = _split2(_log_sigmoid(logits[bi]) * (1.0 / GLA_TAU))
        bcums.append(_dot(tri, la_hi) + _dot(tri, la_lo))

    v, q_rel, k_rel, q_dec, k_dec, carry = [], [], [], [], [], []
    for bi, h in chains:
        hs = slice(h * dk, (h + 1) * dk)
        bcum = bcums[bi][:, hs]
        blast = bcum[L - 1:L, :]
        ref = bcum[L // 2 - 1:L // 2, :]
        qf = q_ref[bi, :, hs].astype(F32) * scale
        kf = k_ref[bi, :, hs].astype(F32)
        v.append(v_ref[bi, :, h * dv:(h + 1) * dv])
        q_rel.append((qf * jnp.exp(bcum - ref)).astype(BF16))
        k_rel.append((kf * jnp.exp(ref - bcum)).astype(BF16))
        q_dec.append((qf * jnp.exp(bcum)).astype(BF16))
        k_dec.append((kf * jnp.exp(blast - bcum)).astype(BF16))
        carry.append(jnp.exp(blast))

    n = len(chains)
    att = [jnp.where(causal, _dot_nt(q_rel[s], k_rel[s]), 0.0).astype(BF16) for s in range(n)]
    o_state = [_dot_nt(q_dec[s], st_ref[s].astype(BF16)) for s in range(n)]
    kv = [_dot_tn(v[s], k_dec[s]) for s in range(n)]

    o_intra = [_dot(att[s], v[s]) for s in range(n)]
    new_st, outs = [], []
    for s, (bi, h) in enumerate(chains):
        o = o_intra[s] + o_state[s]
        o = o * lax.rsqrt(jnp.mean(o * o, axis=-1, keepdims=True) + RMS_EPS) * ng_ref[...]
        r = r_ref[bi, :, h * dv:(h + 1) * dv].astype(F32)
        outs.append((o * (r * _sigmoid(r))).astype(o_ref.dtype))
        new_st.append(st_ref[s] * carry[s] + kv[s])
    st_ref[...] = jnp.stack(new_st)
    o_ref[...] = jnp.stack([jnp.concatenate(outs[bi * nh:(bi + 1) * nh], axis=1) for bi in range(bb)])


def gla_core(proj, a_low, wa_hi, wa_lo, b_a, norm_g, batch, seq, d_model, bb):
    nh, L = GLA_HEADS, CHUNK
    dk = d_model // 2 // nh
    dv = d_model // nh
    nc = seq // L
    qw = nh * dk
    bb = min(bb, batch)
    proj = proj.reshape(batch, seq, 3 * d_model)
    a_low = a_low.reshape(batch, seq, LANES)
    out = pl.pallas_call(
        functools.partial(_gla_kernel, nh=nh, dk=dk, dv=dv, L=L),
        grid=(batch // bb, nc),
        in_specs=[pl.BlockSpec((bb, L, qw), lambda g, c: (g, c, 0)),
                  pl.BlockSpec((bb, L, qw), lambda g, c: (g, c, 1)),
                  pl.BlockSpec((bb, L, d_model), lambda g, c: (g, c, 1)),
                  pl.BlockSpec((bb, L, d_model), lambda g, c: (g, c, 2)),
                  pl.BlockSpec((bb, L, LANES), lambda g, c: (g, c, 0)),
                  pl.BlockSpec((LANES, qw), lambda g, c: (0, 0)),
                  pl.BlockSpec((LANES, qw), lambda g, c: (0, 0)),
                  pl.BlockSpec((1, qw), lambda g, c: (0, 0)),
                  pl.BlockSpec((1, dv), lambda g, c: (0, 0))],
        out_specs=pl.BlockSpec((bb, L, d_model), lambda g, c: (g, c, 0)),
        out_shape=jax.ShapeDtypeStruct((batch, seq, d_model), BF16),
        scratch_shapes=[pltpu.VMEM((bb * nh, dv, dk), F32)],
        compiler_params=_params("parallel", "arbitrary"),
        name="gla_core",
    )(proj, proj, proj, proj, a_low, wa_hi, wa_lo, b_a, norm_g)
    return out.reshape(batch * seq, d_model)


def _xattn_kernel(hb_ref, hf_ref, wq_ref, kv_ref, wo_ref, g_ref, b_ref, of_ref, ob_ref,
                  *, nh, alpha):
    d = wq_ref.shape[1]
    hd = d // nh
    q = (_dot(hb_ref[...], wq_ref[...]) * (hd ** -0.5)).astype(BF16)
    outs = []
    for h in range(nh):
        kh = kv_ref[:, h * hd:(h + 1) * hd]
        vh = kv_ref[:, d + h * hd:d + (h + 1) * hd]
        s = _dot_nt(q[:, h * hd:(h + 1) * hd], kh)
        p = jnp.exp(s - jnp.max(s, axis=-1, keepdims=True))
        l = jnp.sum(p, axis=-1, keepdims=True)
        outs.append((_dot(p.astype(BF16), vh) / l).astype(BF16))
    o = jnp.concatenate(outs, axis=-1)
    z = alpha * hf_ref[...] + _dot(o, wo_ref[...])
    y = _layer_norm(z, g_ref[...], b_ref[...])
    of_ref[...] = y
    ob_ref[...] = y.astype(BF16)


def xattn_block(hb, hf, wq, kv, wo, g, b, batch, seq, n_mem, alpha, tm):
    d = hb.shape[1]
    tm = min(tm, seq)
    nt = seq // tm
    return pl.pallas_call(
        functools.partial(_xattn_kernel, nh=XA_HEADS, alpha=alpha),
        grid=(batch, nt),
        in_specs=[pl.BlockSpec((tm, d), lambda bi, i: (bi * nt + i, 0)),
                  pl.BlockSpec((tm, d), lambda bi, i: (bi * nt + i, 0)),
                  pl.BlockSpec((d, d), lambda bi, i: (0, 0)),
                  pl.BlockSpec((n_mem, 2 * d), lambda bi, i: (bi, 0)),
                  pl.BlockSpec((d, d), lambda bi, i: (0, 0)),
                  pl.BlockSpec((1, d), lambda bi, i: (0, 0)),
                  pl.BlockSpec((1, d), lambda bi, i: (0, 0))],
        out_specs=[pl.BlockSpec((tm, d), lambda bi, i: (bi * nt + i, 0)),
                   pl.BlockSpec((tm, d), lambda bi, i: (bi * nt + i, 0))],
        out_shape=[jax.ShapeDtypeStruct((batch * seq, d), F32),
                   jax.ShapeDtypeStruct((batch * seq, d), BF16)],
        compiler_params=_params("parallel", "parallel"),
        name="xattn_block",
    )(hb, hf, wq, kv, wo, g, b)


def _gelu_tanh(x):
    return 0.5 * x * (1.0 + jnp.tanh(0.7978845608028654 * (x + 0.044715 * (x * x * x))))


def _ffn_kernel(hb_ref, hf_ref, wu_ref, cw_ref, cb_ref, wd_ref, g_ref, b_ref, of_ref, ob_ref, tail_ref,
                *, tm, tf, group, blocks_per_seq, alpha):
    i = pl.program_id(0)
    f = wd_ref.shape[0]

    @pl.when(i % blocks_per_seq == 0)
    def _():
        tail_ref[...] = jnp.zeros_like(tail_ref)

    hb = hb_ref[...]
    row = lax.broadcasted_iota(jnp.int32, (SUBLANES, tf), 0)

    def up(c):
        return [_dot(hb, wu_ref[:, lo:lo + tf]) for lo in (c * tf, f + c * tf)]

    def conv(u, lo):
        cols = slice(lo, lo + tf)
        tail = tail_ref[:, cols]
        tail_ref[:, cols] = u[tm - SUBLANES:, :]
        p1 = tail[SUBLANES - 1:SUBLANES, :]
        p2 = tail[SUBLANES - 2:SUBLANES - 1, :]
        r1 = pltpu.roll(u, 1, 0)
        r2 = pltpu.roll(u, 2, 0)
        top1 = jnp.where(row == 0, p1, r1[:SUBLANES])
        top2 = jnp.where(row == 0, p2, jnp.where(row == 1, p1, r2[:SUBLANES]))
        u1 = jnp.concatenate([top1, r1[SUBLANES:]], axis=0)
        u2 = jnp.concatenate([top2, r2[SUBLANES:]], axis=0)
        c = cw_ref[:, cols]
        return c[0:1, :] * u2 + c[1:2, :] * u1 + c[2:3, :] * u + cb_ref[:, cols]

    n = f // tf
    y = alpha * hf_ref[...]
    u_next = up(0)
    acts, first = [], 0
    for c in range(n):
        u_gate, u_val = u_next
        if c + 1 < n:
            u_next = up(c + 1)
        acts.append((_gelu_tanh(conv(u_gate, c * tf)) * conv(u_val, f + c * tf)).astype(BF16))
        if len(acts) == group or c + 1 == n:
            y = y + _dot(jnp.concatenate(acts, axis=1), wd_ref[first * tf:(c + 1) * tf, :])
            acts, first = [], c + 1
    y = _layer_norm(y, g_ref[...], b_ref[...])
    of_ref[...] = y
    ob_ref[...] = y.astype(BF16)


def ffn_block(hb, hf, w_up, conv_w, conv_b, w_down, g, b, seq, alpha, tm, tf, group):
    t, d = hb.shape
    f = w_down.shape[0]
    tm = min(tm, seq)
    whole = lambda shape: pl.BlockSpec(shape, lambda i: (0, 0), pipeline_mode=pl.Buffered(1))
    return pl.pallas_call(
        functools.partial(_ffn_kernel, tm=tm, tf=tf, group=group, blocks_per_seq=seq // tm, alpha=alpha),
        grid=(t // tm,),
        in_specs=[pl.BlockSpec((tm, d), lambda i: (i, 0)),
                  pl.BlockSpec((tm, d), lambda i: (i, 0)),
                  whole((d, 2 * f)),
                  whole((CONV_W, 2 * f)),
                  whole((1, 2 * f)),
                  whole((f, d)),
                  whole((1, d)),
                  whole((1, d))],
        out_specs=[pl.BlockSpec((tm, d), lambda i: (i, 0)),
                   pl.BlockSpec((tm, d), lambda i: (i, 0))],
        out_shape=[jax.ShapeDtypeStruct((t, d), F32),
                   jax.ShapeDtypeStruct((t, d), BF16)],
        scratch_shapes=[pltpu.VMEM((SUBLANES, 2 * f), F32)],
        compiler_params=_params("arbitrary"),
        name="ffn_block",
    )(hb, hf, w_up, conv_w, conv_b, w_down, g, b)


def _pad_cols(w, width):
    return jnp.pad(w, ((0, 0), (0, width - w.shape[1])))


def kernel(x, mem, ln_g, ln_b, mix_wo, sb_win, fox_win, fox_bf, ml_win, ml_bi, ml_bf,
           gla_win, gla_wa2, gla_ba, gla_norm_g, xa_wq, xa_wkv, xa_wo,
           ffn_up, ffn_conv, ffn_conv_b, ffn_down):
    batch, seq, d = x.shape
    n_mem = mem.shape[1]
    depth = mix_wo.shape[0]
    t = batch * seq
    alpha = (2.0 * depth) ** 0.25
    main = 3 * d

    hf = x.reshape(t, d)
    hb = hf.astype(BF16)
    memb = mem.reshape(batch * n_mem, d).astype(BF16)

    for layer in range(depth):
        kind = layer % 4
        occ = layer // 4
        w_in = (sb_win, fox_win, ml_win, gla_win)[kind][occ]
        proj = linear(hb, w_in[:, :main].astype(BF16), BF16, tm=1024, tn=1024)
        if kind == 0:
            y = sb_attention(proj, batch, seq, d)
        elif kind == 1:
            f_pre = linear(hb, _pad_cols(w_in[:, main:], LANES).astype(BF16), F32, tm=1024, tn=LANES)
            cum = fox_cum(f_pre.reshape(batch, seq, LANES), _pad_cols(fox_bf[occ][None, :], LANES))
            y = fox_attention(proj, cum[:, :, :ATT_HEADS], batch, seq, d)
        elif kind == 2:
            g_pre = linear(hb, _pad_cols(w_in[:, main:], LANES).astype(BF16), F32, tm=1024, tn=LANES)
            gates = g_pre[:, :2 * ML_HEADS] + jnp.concatenate([ml_bi[occ], ml_bf[occ]])[None, :]
            y = mlstm_core(proj, gates.reshape(batch, seq, 2 * ML_HEADS), batch, seq, d, bb=4)
        else:
            a_low = linear(hb, _pad_cols(w_in[:, main:], LANES).astype(BF16), F32, tm=1024, tn=LANES)
            wa = jnp.pad(gla_wa2[occ], ((0, LANES - GLA_RANK), (0, 0)))
            wa_hi = wa.astype(BF16)
            wa_lo = (wa - wa_hi.astype(F32)).astype(BF16)
            y = gla_core(proj, a_low, wa_hi, wa_lo, gla_ba[occ][None, :], gla_norm_g[occ][None, :],
                         batch, seq, d, bb=4)
        hf, hb = linear_res_ln(y, mix_wo[layer].astype(BF16), hf, ln_g[layer, 0][None, :],
                               ln_b[layer, 0][None, :], alpha, tm=512)

        kv = linear(memb, xa_wkv[layer].astype(BF16), BF16, tm=1024, tn=1024)
        hf, hb = xattn_block(hb, hf, xa_wq[layer].astype(BF16), kv, xa_wo[layer].astype(BF16),
                             ln_g[layer, 1][None, :], ln_b[layer, 1][None, :],
                             batch, seq, n_mem, alpha, tm=512)

        hf, hb = ffn_block(hb, hf, ffn_up[layer].astype(BF16), ffn_conv[layer, :, 0, :],
                           ffn_conv_b[layer][None, :], ffn_down[layer].astype(BF16),
                           ln_g[layer, 2][None, :], ln_b[layer, 2][None, :],
                           seq, alpha, tm=512, tf=256, group=4)
    return hf.reshape(batch, seq, d)
```

```python
import functools

import jax
import jax.numpy as jnp
from jax import lax
from jax.experimental import pallas as pl
from jax.experimental.pallas import tpu as pltpu

F32 = jnp.float32
BF16 = jnp.bfloat16

ATT_HEADS = 16
ML_HEADS = 4
GLA_HEADS = 4
GLA_RANK = 16
GLA_TAU = 16.0
XA_HEADS = 4
CHUNK = 64
CONV_W = 3
LN_EPS = 1e-5
RMS_EPS = 1e-6

LANES = 128
SUBLANES = 8
VMEM_LIMIT = 48 * 1024 * 1024

_NT = (((1,), (1,)), ((), ()))
_TN = (((0,), (0,)), ((), ()))


def _params(*sem):
    return pltpu.CompilerParams(dimension_semantics=sem, vmem_limit_bytes=VMEM_LIMIT)


def _dot(a, b):
    return jnp.dot(a, b, preferred_element_type=F32)


def _dot_nt(a, b):
    return lax.dot_general(a, b, _NT, preferred_element_type=F32)


def _dot_tn(a, b):
    return lax.dot_general(a, b, _TN, preferred_element_type=F32)


def _log_sigmoid(x):
    return jnp.minimum(x, 0.0) - jnp.log1p(jnp.exp(-jnp.abs(x)))


def _sigmoid(x):
    return 1.0 / (1.0 + jnp.exp(-x))


def _split2(x):
    hi = x.astype(BF16)
    lo = (x - hi.astype(F32)).astype(BF16)
    return hi, lo


def _split3(x):
    hi = x.astype(BF16)
    r = x - hi.astype(F32)
    mid = r.astype(BF16)
    lo = (r - mid.astype(F32)).astype(BF16)
    return hi, mid, lo


def _layer_norm(z, g, b):
    mu = jnp.mean(z, axis=-1, keepdims=True)
    zc = z - mu
    var = jnp.mean(zc * zc, axis=-1, keepdims=True)
    return zc * lax.rsqrt(var + LN_EPS) * g + b


def _linear_kernel(x_ref, w_ref, o_ref):
    o_ref[...] = _dot(x_ref[...], w_ref[...]).astype(o_ref.dtype)


def linear(x, w, out_dtype, tm, tn):
    m, k = x.shape
    n = w.shape[1]
    tm = min(tm, m)
    tn = min(tn, n)
    return pl.pallas_call(
        _linear_kernel,
        grid=(m // tm, n // tn),
        in_specs=[pl.BlockSpec((tm, k), lambda i, j: (i, 0)),
                  pl.BlockSpec((k, tn), lambda i, j: (0, j))],
        out_specs=pl.BlockSpec((tm, tn), lambda i, j: (i, j)),
        out_shape=jax.ShapeDtypeStruct((m, n), out_dtype),
        compiler_params=_params("parallel", "arbitrary"),
        name="linear",
    )(x, w)


def _head_queries(q_ref, tq, hd):
    lane = lax.broadcasted_iota(jnp.int32, (tq, LANES), 1)
    qs = q_ref[...] * (hd ** -0.5)
    zero = jnp.zeros_like(qs)
    return lane, (jnp.where(lane < hd, qs, zero), jnp.where(lane >= hd, qs, zero))


def _sweep(i, tq, tk, scores, finish, masks, ascending):
    nd = tq // tk
    nfull = i * nd
    per_trip = 2 if nd % 2 == 0 else 1

    def diagonal():
        for c in (range(nd) if ascending else reversed(range(nd))):
            start = pl.multiple_of(i * tq + c * tk, tk)
            finish(scores(start, c * tk), start, c * tk, masks[c])

    def full_body(j, carry):
        blocks = [per_trip * j + u if ascending else nfull - 1 - per_trip * j - u for u in range(per_trip)]
        starts = [pl.multiple_of(kb * tk, tk) for kb in blocks]
        z = [scores(start, 0) for start in starts]
        for start, zz in zip(starts, z):
            finish(zz, start, 0, None)
        return carry

    if ascending:
        lax.fori_loop(0, nfull // per_trip, full_body, 0)
        diagonal()
    else:
        diagonal()
        lax.fori_loop(0, nfull // per_trip, full_body, 0)


def _diag_masks(tq, tk, inclusive):
    masks = []
    for c in range(tq // tk):
        n = tq - c * tk
        col = lax.broadcasted_iota(jnp.int32, (n, tk), 1)
        row = lax.broadcasted_iota(jnp.int32, (n, tk), 0)
        masks.append(col <= row if inclusive else col < row)
    return masks


def _sb_kernel(q_ref, k_ref, v_ref, o_ref, acc_ref, run_ref, *, tq, tk, hd):
    i = pl.program_id(2)
    lane, q_heads = _head_queries(q_ref, tq, hd)
    r = lax.broadcasted_iota(jnp.int32, (tk, tk), 0)
    c = lax.broadcasted_iota(jnp.int32, (tk, tk), 1)
    neg_later = jnp.where(r > c, -1.0, 0.0).astype(BF16)
    acc_ref[...] = jnp.zeros_like(acc_ref)
    run_ref[...] = jnp.zeros_like(run_ref)

    def scores(start, lo):
        ks = k_ref[pl.ds(start, tk), :]
        return [_dot_nt(q_heads[hh][lo:], ks) for hh in range(2)]

    def finish(z, start, lo, mask):
        vs = v_ref[pl.ds(start, tk), :]
        for hh in range(2):
            zb = z[hh].astype(BF16)
            soft = jnp.log(1.0 + jnp.exp(-jnp.abs(zb)))
            drop = jnp.maximum(zb, 0.0) + soft
            if mask is not None:
                drop = jnp.where(mask, drop, jnp.zeros_like(drop))
            suffix = _dot(drop, neg_later)
            total = suffix[:, 0:1] - drop[:, 0:1].astype(F32)
            run = run_ref[hh, lo:, :]
            log_sig = jnp.minimum(z[hh], 0.0) - soft.astype(F32)
            w = jnp.exp(log_sig + suffix + jnp.concatenate([run] * (tk // LANES), axis=1))
            if mask is not None:
                w = jnp.where(mask, w, 0.0)
            acc_ref[hh, lo:, :] += _dot(w.astype(BF16), vs)
            run_ref[hh, lo:, :] = run + total

    _sweep(i, tq, tk, scores, finish, _diag_masks(tq, tk, inclusive=False), ascending=False)
    o_ref[...] = jnp.where(lane < hd, acc_ref[0], acc_ref[1]).astype(o_ref.dtype)


def sb_attention(qkv, batch, seq, d_model, tq=512, tk=256):
    hd = d_model // ATT_HEADS
    npair = d_model // LANES
    tq = min(tq, seq)
    nq = seq // tq
    return pl.pallas_call(
        functools.partial(_sb_kernel, tq=tq, tk=tk, hd=hd),
        grid=(batch, npair, nq),
        in_specs=[pl.BlockSpec((tq, LANES), lambda b, p, i: (b * nq + i, p)),
                  pl.BlockSpec((seq, LANES), lambda b, p, i: (b, npair + p)),
                  pl.BlockSpec((seq, LANES), lambda b, p, i: (b, 2 * npair + p))],
        out_specs=pl.BlockSpec((tq, LANES), lambda b, p, i: (b * nq + i, p)),
        out_shape=jax.ShapeDtypeStruct((batch * seq, d_model), BF16),
        scratch_shapes=[pltpu.VMEM((2, tq, LANES), F32), pltpu.VMEM((2, tq, LANES), F32)],
        compiler_params=_params("parallel", "parallel", "arbitrary"),
        name="sb_attention",
    )(qkv, qkv, qkv)


def _fox_cum_kernel(f_ref, b_ref, o_ref, *, seq, blk):
    row = lax.broadcasted_iota(jnp.int32, (blk, blk), 0)
    col = lax.broadcasted_iota(jnp.int32, (blk, blk), 1)
    tri = jnp.where(col <= row, 1.0, 0.0).astype(BF16)
    run = jnp.zeros((1, f_ref.shape[-1]), F32)
    for c in range(seq // blk):
        lf = _log_sigmoid(f_ref[0, c * blk:(c + 1) * blk, :] + b_ref[...])
        hi, mid, lo = _split3(lf)
        cs = _dot(tri, hi) + _dot(tri, mid) + _dot(tri, lo) + run
        o_ref[0, c * blk:(c + 1) * blk, :] = cs
        run = cs[blk - 1:blk, :]


def fox_cum(f_pre, b_f, blk=256):
    batch, seq, nh = f_pre.shape
    blk = min(blk, seq)
    return pl.pallas_call(
        functools.partial(_fox_cum_kernel, seq=seq, blk=blk),
        grid=(batch,),
        in_specs=[pl.BlockSpec((1, seq, nh), lambda b: (b, 0, 0)),
                  pl.BlockSpec((1, nh), lambda b: (0, 0))],
        out_specs=pl.BlockSpec((1, seq, nh), lambda b: (b, 0, 0)),
        out_shape=jax.ShapeDtypeStruct((batch, seq, nh), F32),
        compiler_params=_params("parallel"),
        name="fox_cum",
    )(f_pre, b_f)


def _fox_kernel(q_ref, k_ref, v_ref, cr_ref, o_ref, acc_ref, m_ref, *, tq, tk, hd):
    i = pl.program_id(2)
    lane, q_heads = _head_queries(q_ref, tq, hd)
    klane = lax.broadcasted_iota(jnp.int32, (tk, LANES), 1)
    in_head = (klane < hd, klane >= hd)
    acc_ref[...] = jnp.zeros_like(acc_ref)
    m_ref[...] = jnp.full_like(m_ref, -jnp.inf)

    def scores(start, lo):
        ks = k_ref[pl.ds(start, tk), :]
        return [_dot_nt(q_heads[hh][lo:], ks) for hh in range(2)]

    def finish(s, start, lo, mask):
        vs = v_ref[pl.ds(start, tk), :]
        for hh in range(2):
            vh = jnp.where(in_head[hh], vs, jnp.ones_like(vs))
            cum_row = cr_ref[0, 0, hh:hh + 1, pl.ds(start, tk)]
            z = s[hh] - cum_row
            if mask is not None:
                z = jnp.where(mask, z, -jnp.inf)
            m_prev = m_ref[hh, lo:, :]
            m_new = jnp.maximum(m_prev, jnp.max(z, axis=-1, keepdims=True))
            a = jnp.exp(m_prev - m_new)
            p = jnp.exp(z - jnp.concatenate([m_new] * (tk // LANES), axis=1)).astype(BF16)
            m_ref[hh, lo:, :] = m_new
            acc_ref[hh, lo:, :] = a * acc_ref[hh, lo:, :] + _dot(p, vh)

    _sweep(i, tq, tk, scores, finish, _diag_masks(tq, tk, inclusive=True), ascending=True)
    acc0 = acc_ref[0]
    acc1 = acc_ref[1]
    o = jnp.where(lane < hd, acc0 / acc0[:, hd:hd + 1], acc1 / acc1[:, 0:1])
    o_ref[...] = o.astype(o_ref.dtype)


def fox_attention(qkv, cum, batch, seq, d_model, tq=512, tk=256):
    hd = d_model // ATT_HEADS
    npair = d_model // LANES
    tq = min(tq, seq)
    nq = seq // tq
    cum_row = cum.reshape(batch, seq, npair, 2).transpose(0, 2, 3, 1)
    return pl.pallas_call(
        functools.partial(_fox_kernel, tq=tq, tk=tk, hd=hd),
        grid=(batch, npair, nq),
        in_specs=[pl.BlockSpec((tq, LANES), lambda b, p, i: (b * nq + i, p)),
                  pl.BlockSpec((seq, LANES), lambda b, p, i: (b, npair + p)),
                  pl.BlockSpec((seq, LANES), lambda b, p, i: (b, 2 * npair + p)),
                  pl.BlockSpec((1, 1, 2, seq), lambda b, p, i: (b, p, 0, 0))],
        out_specs=pl.BlockSpec((tq, LANES), lambda b, p, i: (b * nq + i, p)),
        out_shape=jax.ShapeDtypeStruct((batch * seq, d_model), BF16),
        scratch_shapes=[pltpu.VMEM((2, tq, LANES), F32), pltpu.VMEM((2, tq, LANES), F32)],
        compiler_params=_params("parallel", "parallel", "arbitrary"),
        name="fox_attention",
    )(qkv, qkv, qkv, cum_row)


def _mlstm_kernel(q_ref, k_ref, v_ref, og_ref, gc_ref, gr_ref, o_ref, c_ref, n_ref, m_ref,
                  *, nh, dk, dv, L):
    @pl.when(pl.program_id(1) == 0)
    def _():
        c_ref[...] = jnp.zeros_like(c_ref)
        n_ref[...] = jnp.zeros_like(n_ref)
        m_ref[...] = jnp.zeros_like(m_ref)

    row = lax.broadcasted_iota(jnp.int32, (L, L), 0)
    col = lax.broadcasted_iota(jnp.int32, (L, L), 1)
    causal = col <= row
    scale = dk ** -0.5
    bb = q_ref.shape[0]
    chains = [(bi, h) for bi in range(bb) for h in range(nh)]
    q = [q_ref[bi, :, h * dk:(h + 1) * dk] for bi, h in chains]
    k = [k_ref[bi, :, h * dk:(h + 1) * dk] for bi, h in chains]
    v = [v_ref[bi, :, h * dv:(h + 1) * dv] for bi, h in chains]

    qk_raw = [_dot_nt(q[s], k[s]) for s in range(len(chains))]
    q_state = [_dot(q[s], c_ref[s].astype(BF16)) for s in range(len(chains))]

    w_intra, w_inter, m_t, m_new, decay, kw = [], [], [], [], [], []
    for s, (bi, h) in enumerate(chains):
        gcol = gc_ref[bi]
        grow = gr_ref[bi, 0]
        i_col = gcol[:, h:h + 1]
        i_row = grow[h:h + 1, :]
        lf_col = _log_sigmoid(gcol[:, nh + h:nh + h + 1])
        lf_row = _log_sigmoid(grow[nh + h:nh + h + 1, :])
        bcum_col = jnp.sum(jnp.where(causal, lf_row, 0.0), axis=1, keepdims=True)
        bcum_row = jnp.sum(jnp.where(row <= col, lf_col, 0.0), axis=0, keepdims=True)
        m_prev = m_ref[s]
        d = jnp.where(causal, bcum_col - bcum_row + i_row, -jnp.inf)
        inter = bcum_col + m_prev
        mt = jnp.maximum(inter, jnp.max(d, axis=1, keepdims=True))
        mn = mt[L - 1:L, :]
        blast = bcum_col[L - 1:L, :]
        m_t.append(mt)
        m_new.append(mn)
        w_intra.append(jnp.exp(d - mt))
        w_inter.append(jnp.exp(inter - mt))
        decay.append(jnp.exp(blast + m_prev - mn))
        kw.append((k[s].astype(F32) * scale) * jnp.exp(blast - bcum_col + i_col - mn))

    qk = [qk_raw[s] * (scale * w_intra[s]) for s in range(len(chains))]
    intra = [_dot(qk[s].astype(BF16), v[s]) for s in range(len(chains))]
    kv = [_dot_tn(kw[s].astype(BF16), v[s]) for s in range(len(chains))]

    new_c, new_n, outs = [], [], []
    for s, (bi, h) in enumerate(chains):
        n_row = n_ref[s]
        num = w_inter[s] * q_state[s] + intra[s]
        den = (w_inter[s] * jnp.sum(q[s].astype(F32) * n_row, axis=1, keepdims=True)
               + jnp.sum(qk[s], axis=1, keepdims=True))
        h_out = num / jnp.maximum(jnp.abs(den), jnp.exp(-m_t[s]))
        new_c.append(decay[s] * c_ref[s] + kv[s])
        new_n.append(decay[s] * n_row + jnp.sum(kw[s], axis=0, keepdims=True))
        gate = _sigmoid(og_ref[bi, :, h * dv:(h + 1) * dv].astype(F32))
        outs.append((h_out * gate).astype(o_ref.dtype))
    c_ref[...] = jnp.stack(new_c)
    n_ref[...] = jnp.stack(new_n)
    m_ref[...] = jnp.stack(m_new)
    o_ref[...] = jnp.stack([jnp.concatenate(outs[bi * nh:(bi + 1) * nh], axis=1) for bi in range(bb)])


def mlstm_core(proj, gates, batch, seq, d_model, bb):
    nh, L = ML_HEADS, CHUNK
    dk = d_model // 2 // nh
    dv = d_model // nh
    nc = seq // L
    qw = nh * dk
    bb = min(bb, batch)
    proj = proj.reshape(batch, seq, 3 * d_model)
    gates_row = gates.reshape(batch, nc, L, 2 * nh).transpose(0, 1, 3, 2)
    out = pl.pallas_call(
        functools.partial(_mlstm_kernel, nh=nh, dk=dk, dv=dv, L=L),
        grid=(batch // bb, nc),
        in_specs=[pl.BlockSpec((bb, L, qw), lambda g, c: (g, c, 0)),
                  pl.BlockSpec((bb, L, qw), lambda g, c: (g, c, 1)),
                  pl.BlockSpec((bb, L, d_model), lambda g, c: (g, c, 1)),
                  pl.BlockSpec((bb, L, d_model), lambda g, c: (g, c, 2)),
                  pl.BlockSpec((bb, L, 2 * nh), lambda g, c: (g, c, 0)),
                  pl.BlockSpec((bb, 1, 2 * nh, L), lambda g, c: (g, c, 0, 0))],
        out_specs=pl.BlockSpec((bb, L, d_model), lambda g, c: (g, c, 0)),
        out_shape=jax.ShapeDtypeStruct((batch, seq, d_model), BF16),
        scratch_shapes=[pltpu.VMEM((bb * nh, dk, dv), F32),
                        pltpu.VMEM((bb * nh, 1, dk), F32),
                        pltpu.VMEM((bb * nh, 1, 1), F32)],
        compiler_params=_params("parallel", "arbitrary"),
        name="mlstm_core",
    )(proj, proj, proj, proj, gates, gates_row)
    return out.reshape(batch * seq, d_model)


def _gla_kernel(q_ref, k_ref, v_ref, r_ref, a_ref, wh_ref, wl_ref, ba_ref, ng_ref, o_ref, st_ref,
                *, nh, dk, dv, L):
    @pl.when(pl.program_id(1) == 0)
    def _():
        st_ref[...] = jnp.zeros_like(st_ref)

    row = lax.broadcasted_iota(jnp.int32, (L, L), 0)
    col = lax.broadcasted_iota(jnp.int32, (L, L), 1)
    causal = col <= row
    tri = jnp.where(causal, 1.0, 0.0).astype(BF16)
    scale = dk ** -0.5
    bb = q_ref.shape[0]
    chains = [(bi, h) for bi in range(bb) for h in range(nh)]

    logits = []
    for bi in range(bb):
        a_hi, a_lo = _split2(a_ref[bi])
        wh = wh_ref[...]
        logits.append(_dot(a_hi, wh) + _dot(a_lo, wh) + _dot(a_hi, wl_ref[...]) + ba_ref[...])

    bcums = []
    for bi in range(bb):
        la_hi, la_lo = _split2(_log_sigmoid(logits[bi]) * (1.0 / GLA_TAU))
        bcums.append(_dot(tri, la_hi) + _dot(tri, la_lo))

    v, q_rel, k_rel, q_dec, k_dec, carry = [], [], [], [], [], []
    for bi, h in chains:
        hs = slice(h * dk, (h + 1) * dk)
        bcum = bcums[bi][:, hs]
        blast = bcum[L - 1:L, :]
        ref = bcum[L // 2 - 1:L // 2, :]
        qf = q_ref[bi, :, hs].astype(F32) * scale
        kf = k_ref[bi, :, hs].astype(F32)
        v.append(v_ref[bi, :, h * dv:(h + 1) * dv])
        q_rel.append((qf * jnp.exp(bcum - ref)).astype(BF16))
        k_rel.append((kf * jnp.exp(ref - bcum)).astype(BF16))
        q_dec.append((qf * jnp.exp(bcum)).astype(BF16))
        k_dec.append((kf * jnp.exp(blast - bcum)).astype(BF16))
        carry.append(jnp.exp(blast))

    n = len(chains)
    att = [jnp.where(causal, _dot_nt(q_rel[s], k_rel[s]), 0.0).astype(BF16) for s in range(n)]
    o_state = [_dot_nt(q_dec[s], st_ref[s].astype(BF16)) for s in range(n)]
    kv = [_dot_tn(v[s], k_dec[s]) for s in range(n)]

    o_intra = [_dot(att[s], v[s]) for s in range(n)]
    new_st, outs = [], []
    for s, (bi, h) in enumerate(chains):
        o = o_intra[s] + o_state[s]
        o = o * lax.rsqrt(jnp.mean(o * o, axis=-1, keepdims=True) + RMS_EPS) * ng_ref[...]
        r = r_ref[bi, :, h * dv:(h + 1) * dv].astype(F32)
        outs.append((o * (r * _sigmoid(r))).astype(o_ref.dtype))
        new_st.append(st_ref[s] * carry[s] + kv[s])
    st_ref[...] = jnp.stack(new_st)
    o_ref[...] = jnp.stack([jnp.concatenate(outs[bi * nh:(bi + 1) * nh], axis=1) for bi in range(bb)])


def gla_core(proj, a_low, wa_hi, wa_lo, b_a, norm_g, batch, seq, d_model, bb):
    nh, L = GLA_HEADS, CHUNK
    dk = d_model // 2 // nh
    dv = d_model // nh
    nc = seq // L
    qw = nh * dk
    bb = min(bb, batch)
    proj = proj.reshape(batch, seq, 3 * d_model)
    a_low = a_low.reshape(batch, seq, LANES)
    out = pl.pallas_call(
        functools.partial(_gla_kernel, nh=nh, dk=dk, dv=dv, L=L),
        grid=(batch // bb, nc),
        in_specs=[pl.BlockSpec((bb, L, qw), lambda g, c: (g, c, 0)),
                  pl.BlockSpec((bb, L, qw), lambda g, c: (g, c, 1)),
                  pl.BlockSpec((bb, L, d_model), lambda g, c: (g, c, 1)),
                  pl.BlockSpec((bb, L, d_model), lambda g, c: (g, c, 2)),
                  pl.BlockSpec((bb, L, LANES), lambda g, c: (g, c, 0)),
                  pl.BlockSpec((LANES, qw), lambda g, c: (0, 0)),
                  pl.BlockSpec((LANES, qw), lambda g, c: (0, 0)),
                  pl.BlockSpec((1, qw), lambda g, c: (0, 0)),
                  pl.BlockSpec((1, dv), lambda g, c: (0, 0))],
        out_specs=pl.BlockSpec((bb, L, d_model), lambda g, c: (g, c, 0)),
        out_shape=jax.ShapeDtypeStruct((batch, seq, d_model), BF16),
        scratch_shapes=[pltpu.VMEM((bb * nh, dv, dk), F32)],
        compiler_params=_params("parallel", "arbitrary"),
        name="gla_core",
    )(proj, proj, proj, proj, a_low, wa_hi, wa_lo, b_a, norm_g)
    return out.reshape(batch * seq, d_model)


def _mix_xattn_kernel(y_ref, hf_ref, wm_ref, g0_ref, b0_ref, wq_ref, kv_ref, wo_ref, g1_ref, b1_ref,
                      of_ref, ob_ref, *, nh, alpha):
    d = wq_ref.shape[1]
    hd = d // nh
    h1 = _layer_norm(alpha * hf_ref[...] + _dot(y_ref[...], wm_ref[...]), g0_ref[...], b0_ref[...])
    q = (_dot(h1.astype(BF16), wq_ref[...]) * (hd ** -0.5)).astype(BF16)
    outs = []
    for h in range(nh):
        kh = kv_ref[:, h * hd:(h + 1) * hd]
        vh = kv_ref[:, d + h * hd:d + (h + 1) * hd]
        s = _dot_nt(q[:, h * hd:(h + 1) * hd], kh)
        p = jnp.exp(s - jnp.max(s, axis=-1, keepdims=True))
        l = jnp.sum(p, axis=-1, keepdims=True)
        outs.append((_dot(p.astype(BF16), vh) / l).astype(BF16))
    o = jnp.concatenate(outs, axis=-1)
    h2 = _layer_norm(alpha * h1 + _dot(o, wo_ref[...]), g1_ref[...], b1_ref[...])
    of_ref[...] = h2
    ob_ref[...] = h2.astype(BF16)


def mix_xattn_block(y, hf, w_mix, g0, b0, wq, kv, wo, g1, b1, batch, seq, n_mem, alpha, tm):
    d = hf.shape[1]
    tm = min(tm, seq)
    nt = seq // tm
    rows = pl.BlockSpec((tm, d), lambda bi, i: (bi * nt + i, 0))
    whole = lambda shape: pl.BlockSpec(shape, lambda bi, i: (0, 0), pipeline_mode=pl.Buffered(1))
    return pl.pallas_call(
        functools.partial(_mix_xattn_kernel, nh=XA_HEADS, alpha=alpha),
        grid=(batch, nt),
        in_specs=[rows, rows, whole((d, d)), whole((1, d)), whole((1, d)),
                  whole((d, d)), pl.BlockSpec((n_mem, 2 * d), lambda bi, i: (bi, 0)), whole((d, d)),
                  whole((1, d)), whole((1, d))],
        out_specs=[rows, rows],
        out_shape=[jax.ShapeDtypeStruct((batch * seq, d), F32),
                   jax.ShapeDtypeStruct((batch * seq, d), BF16)],
        compiler_params=_params("parallel", "parallel"),
        name="mix_xattn_block",
    )(y, hf, w_mix, g0, b0, wq, kv, wo, g1, b1)


def _gelu_tanh(x):
    return 0.5 * x * (1.0 + jnp.tanh(0.7978845608028654 * (x + 0.044715 * (x * x * x))))


def _ffn_kernel(hb_ref, hf_ref, wu_ref, cw_ref, cb_ref, wd_ref, g_ref, b_ref, of_ref, ob_ref, tail_ref,
                *, tm, tf, group, blocks_per_seq, alpha):
    i = pl.program_id(0)
    f = wd_ref.shape[0]

    @pl.when(i % blocks_per_seq == 0)
    def _():
        tail_ref[...] = jnp.zeros_like(tail_ref)

    hb = hb_ref[...]
    row = lax.broadcasted_iota(jnp.int32, (SUBLANES, tf), 0)

    def up(c):
        return [_dot(hb, wu_ref[:, lo:lo + tf]) for lo in (c * tf, f + c * tf)]

    def conv(u, lo):
        cols = slice(lo, lo + tf)
        tail = tail_ref[:, cols]
        tail_ref[:, cols] = u[tm - SUBLANES:, :]
        p1 = tail[SUBLANES - 1:SUBLANES, :]
        p2 = tail[SUBLANES - 2:SUBLANES - 1, :]
        r1 = pltpu.roll(u, 1, 0)
        r2 = pltpu.roll(u, 2, 0)
        top1 = jnp.where(row == 0, p1, r1[:SUBLANES])
        top2 = jnp.where(row == 0, p2, jnp.where(row == 1, p1, r2[:SUBLANES]))
        u1 = jnp.concatenate([top1, r1[SUBLANES:]], axis=0)
        u2 = jnp.concatenate([top2, r2[SUBLANES:]], axis=0)
        c = cw_ref[:, cols]
        return c[0:1, :] * u2 + c[1:2, :] * u1 + c[2:3, :] * u + cb_ref[:, cols]

    n = f // tf
    y = alpha * hf_ref[...]
    u_next = up(0)
    acts, first = [], 0
    for c in range(n):
        u_gate, u_val = u_next
        if c + 1 < n:
            u_next = up(c + 1)
        acts.append((_gelu_tanh(conv(u_gate, c * tf)) * conv(u_val, f + c * tf)).astype(BF16))
        if len(acts) == group or c + 1 == n:
            y = y + _dot(jnp.concatenate(acts, axis=1), wd_ref[first * tf:(c + 1) * tf, :])
            acts, first = [], c + 1
    y = _layer_norm(y, g_ref[...], b_ref[...])
    of_ref[...] = y
    ob_ref[...] = y.astype(BF16)


def ffn_block(hb, hf, w_up, conv_w, conv_b, w_down, g, b, seq, alpha, tm, tf, group):
    t, d = hb.shape
    f = w_down.shape[0]
    tm = min(tm, seq)
    whole = lambda shape: pl.BlockSpec(shape, lambda i: (0, 0), pipeline_mode=pl.Buffered(1))
    return pl.pallas_call(
        functools.partial(_ffn_kernel, tm=tm, tf=tf, group=group, blocks_per_seq=seq // tm, alpha=alpha),
        grid=(t // tm,),
        in_specs=[pl.BlockSpec((tm, d), lambda i: (i, 0)),
                  pl.BlockSpec((tm, d), lambda i: (i, 0)),
                  whole((d, 2 * f)),
                  whole((CONV_W, 2 * f)),
                  whole((1, 2 * f)),
                  whole((f, d)),
                  whole((1, d)),
                  whole((1, d))],
        out_specs=[pl.BlockSpec((tm, d), lambda i: (i, 0)),
                   pl.BlockSpec((tm, d), lambda i: (i, 0))],
        out_shape=[jax.ShapeDtypeStruct((t, d), F32),
                   jax.ShapeDtypeStruct((t, d), BF16)],
        scratch_shapes=[pltpu.VMEM((SUBLANES, 2 * f), F32)],
        compiler_params=_params("arbitrary"),
        name="ffn_block",
    )(hb, hf, w_up, conv_w, conv_b, w_down, g, b)


def _pad_cols(w, width):
    return jnp.pad(w, ((0, 0), (0, width - w.shape[1])))


def kernel(x, mem, ln_g, ln_b, mix_wo, sb_win, fox_win, fox_bf, ml_win, ml_bi, ml_bf,
           gla_win, gla_wa2, gla_ba, gla_norm_g, xa_wq, xa_wkv, xa_wo,
           ffn_up, ffn_conv, ffn_conv_b, ffn_down):
    batch, seq, d = x.shape
    n_mem = mem.shape[1]
    depth = mix_wo.shape[0]
    t = batch * seq
    alpha = (2.0 * depth) ** 0.25
    main = 3 * d

    hf = x.reshape(t, d)
    hb = hf.astype(BF16)
    memb = mem.reshape(batch * n_mem, d).astype(BF16)

    for layer in range(depth):
        kind = layer % 4
        occ = layer // 4
        w_in = (sb_win, fox_win, ml_win, gla_win)[kind][occ]
        proj = linear(hb, w_in[:, :main].astype(BF16), BF16, tm=1024, tn=1024)
        if kind == 0:
            y = sb_attention(proj, batch, seq, d)
        elif kind == 1:
            f_pre = linear(hb, _pad_cols(w_in[:, main:], LANES).astype(BF16), F32, tm=1024, tn=LANES)
            cum = fox_cum(f_pre.reshape(batch, seq, LANES), _pad_cols(fox_bf[occ][None, :], LANES))
            y = fox_attention(proj, cum[:, :, :ATT_HEADS], batch, seq, d)
        elif kind == 2:
            g_pre = linear(hb, _pad_cols(w_in[:, main:], LANES).astype(BF16), F32, tm=1024, tn=LANES)
            gates = g_pre[:, :2 * ML_HEADS] + jnp.concatenate([ml_bi[occ], ml_bf[occ]])[None, :]
            y = mlstm_core(proj, gates.reshape(batch, seq, 2 * ML_HEADS), batch, seq, d, bb=4)
        else:
            a_low = linear(hb, _pad_cols(w_in[:, main:], LANES).astype(BF16), F32, tm=1024, tn=LANES)
            wa = jnp.pad(gla_wa2[occ], ((0, LANES - GLA_RANK), (0, 0)))
            wa_hi = wa.astype(BF16)
            wa_lo = (wa - wa_hi.astype(F32)).astype(BF16)
            y = gla_core(proj, a_low, wa_hi, wa_lo, gla_ba[occ][None, :], gla_norm_g[occ][None, :],
                         batch, seq, d, bb=4)
        kv = linear(memb, xa_wkv[layer].astype(BF16), BF16, tm=1024, tn=1024)
        hf, hb = mix_xattn_block(y, hf, mix_wo[layer].astype(BF16), ln_g[layer, 0][None, :], ln_b[layer, 0][None, :],
                                 xa_wq[layer].astype(BF16), kv, xa_wo[layer].astype(BF16),
                                 ln_g[layer, 1][None, :], ln_b[layer, 1][None, :],
                                 batch, seq, n_mem, alpha, tm=512)

        hf, hb = ffn_block(hb, hf, ffn_up[layer].astype(BF16), ffn_conv[layer, :, 0, :],
                           ffn_conv_b[layer][None, :], ffn_down[layer].astype(BF16),
                           ln_g[layer, 2][None, :], ln_b[layer, 2][None, :],
                           seq, alpha, tm=512, tf=256, group=4)
    return hf.reshape(batch, seq, d)
```

```python
import functools

import jax
import jax.numpy as jnp
from jax import lax
from jax.experimental import pallas as pl
from jax.experimental.pallas import tpu as pltpu

F32 = jnp.float32
BF16 = jnp.bfloat16

ATT_HEADS = 16
ML_HEADS = 4
GLA_HEADS = 4
GLA_RANK = 16
GLA_TAU = 16.0
XA_HEADS = 4
CHUNK = 64
CONV_W = 3
LN_EPS = 1e-5
RMS_EPS = 1e-6

LANES = 128
SUBLANES = 8
VMEM_LIMIT = 48 * 1024 * 1024

_NT = (((1,), (1,)), ((), ()))
_TN = (((0,), (0,)), ((), ()))


def _params(*sem):
    return pltpu.CompilerParams(dimension_semantics=sem, vmem_limit_bytes=VMEM_LIMIT)


def _dot(a, b):
    return jnp.dot(a, b, preferred_element_type=F32)


def _dot_nt(a, b):
    return lax.dot_general(a, b, _NT, preferred_element_type=F32)


def _dot_tn(a, b):
    return lax.dot_general(a, b, _TN, preferred_element_type=F32)


def _log_sigmoid(x):
    return jnp.minimum(x, 0.0) - jnp.log1p(jnp.exp(-jnp.abs(x)))


def _sigmoid(x):
    return 1.0 / (1.0 + jnp.exp(-x))


def _split2(x):
    hi = x.astype(BF16)
    lo = (x - hi.astype(F32)).astype(BF16)
    return hi, lo


def _split3(x):
    hi = x.astype(BF16)
    r = x - hi.astype(F32)
    mid = r.astype(BF16)
    lo = (r - mid.astype(F32)).astype(BF16)
    return hi, mid, lo


def _layer_norm(z, g, b):
    mu = jnp.mean(z, axis=-1, keepdims=True)
    zc = z - mu
    var = jnp.mean(zc * zc, axis=-1, keepdims=True)
    return zc * lax.rsqrt(var + LN_EPS) * g + b


def _linear_kernel(x_ref, w_ref, o_ref):
    o_ref[...] = _dot(x_ref[...].astype(BF16), w_ref[...]).astype(o_ref.dtype)


def linear(x, w, out_dtype, tm, tn):
    m, k = x.shape
    n = w.shape[1]
    tm = min(tm, m)
    tn = min(tn, n)
    return pl.pallas_call(
        _linear_kernel,
        grid=(m // tm, n // tn),
        in_specs=[pl.BlockSpec((tm, k), lambda i, j: (i, 0)),
                  pl.BlockSpec((k, tn), lambda i, j: (0, j))],
        out_specs=pl.BlockSpec((tm, tn), lambda i, j: (i, j)),
        out_shape=jax.ShapeDtypeStruct((m, n), out_dtype),
        compiler_params=_params("parallel", "arbitrary"),
        name="linear",
    )(x, w)


def _in_proj_kernel(x_ref, w_ref, wg_ref, o_ref, og_ref):
    x = x_ref[...].astype(BF16)
    o_ref[...] = _dot(x, w_ref[...]).astype(o_ref.dtype)

    @pl.when(pl.program_id(1) == 0)
    def _():
        og_ref[...] = _dot(x, wg_ref[...])


def in_proj(x, w, wg, tm, tn):
    m, k = x.shape
    n = w.shape[1]
    tm = min(tm, m)
    tn = min(tn, n)
    return pl.pallas_call(
        _in_proj_kernel,
        grid=(m // tm, n // tn),
        in_specs=[pl.BlockSpec((tm, k), lambda i, j: (i, 0)),
                  pl.BlockSpec((k, tn), lambda i, j: (0, j)),
                  pl.BlockSpec((k, LANES), lambda i, j: (0, 0))],
        out_specs=[pl.BlockSpec((tm, tn), lambda i, j: (i, j)),
                   pl.BlockSpec((tm, LANES), lambda i, j: (i, 0))],
        out_shape=[jax.ShapeDtypeStruct((m, n), BF16),
                   jax.ShapeDtypeStruct((m, LANES), F32)],
        compiler_params=_params("parallel", "arbitrary"),
        name="in_proj",
    )(x, w, wg)


def _head_queries(q_ref, tq, hd):
    lane = lax.broadcasted_iota(jnp.int32, (tq, LANES), 1)
    qs = q_ref[...] * (hd ** -0.5)
    zero = jnp.zeros_like(qs)
    return lane, (jnp.where(lane < hd, qs, zero), jnp.where(lane >= hd, qs, zero))


def _sweep(i, tq, tk, scores, finish, masks, ascending):
    nd = tq // tk
    nfull = i * nd
    per_trip = 2 if nd % 2 == 0 else 1

    def diagonal():
        for c in (range(nd) if ascending else reversed(range(nd))):
            start = pl.multiple_of(i * tq + c * tk, tk)
            finish(scores(start, c * tk), start, c * tk, masks[c])

    def full_body(j, carry):
        blocks = [per_trip * j + u if ascending else nfull - 1 - per_trip * j - u for u in range(per_trip)]
        starts = [pl.multiple_of(kb * tk, tk) for kb in blocks]
        z = [scores(start, 0) for start in starts]
        for start, zz in zip(starts, z):
            finish(zz, start, 0, None)
        return carry

    if ascending:
        lax.fori_loop(0, nfull // per_trip, full_body, 0)
        diagonal()
    else:
        diagonal()
        lax.fori_loop(0, nfull // per_trip, full_body, 0)


def _diag_masks(tq, tk, inclusive):
    masks = []
    for c in range(tq // tk):
        n = tq - c * tk
        col = lax.broadcasted_iota(jnp.int32, (n, tk), 1)
        row = lax.broadcasted_iota(jnp.int32, (n, tk), 0)
        masks.append(col <= row if inclusive else col < row)
    return masks


def _sb_kernel(q_ref, k_ref, v_ref, o_ref, acc_ref, run_ref, *, tq, tk, hd):
    i = pl.program_id(2)
    lane, q_heads = _head_queries(q_ref, tq, hd)
    r = lax.broadcasted_iota(jnp.int32, (tk, tk), 0)
    c = lax.broadcasted_iota(jnp.int32, (tk, tk), 1)
    neg_later = jnp.where(r > c, -1.0, 0.0).astype(BF16)
    acc_ref[...] = jnp.zeros_like(acc_ref)
    run_ref[...] = jnp.zeros_like(run_ref)

    def scores(start, lo):
        ks = k_ref[pl.ds(start, tk), :]
        return [_dot_nt(q_heads[hh][lo:], ks) for hh in range(2)]

    def finish(z, start, lo, mask):
        vs = v_ref[pl.ds(start, tk), :]
        for hh in range(2):
            zb = z[hh].astype(BF16)
            soft = jnp.log(1.0 + jnp.exp(-jnp.abs(zb)))
            drop = jnp.maximum(zb, 0.0) + soft
            if mask is not None:
                drop = jnp.where(mask, drop, jnp.zeros_like(drop))
            suffix = _dot(drop, neg_later)
            total = suffix[:, 0:1] - drop[:, 0:1].astype(F32)
            run = run_ref[hh, lo:, :]
            log_sig = jnp.minimum(z[hh], 0.0) - soft.astype(F32)
            w = jnp.exp(log_sig + suffix + jnp.concatenate([run] * (tk // LANES), axis=1))
            if mask is not None:
                w = jnp.where(mask, w, 0.0)
            acc_ref[hh, lo:, :] += _dot(w.astype(BF16), vs)
            run_ref[hh, lo:, :] = run + total

    _sweep(i, tq, tk, scores, finish, _diag_masks(tq, tk, inclusive=False), ascending=False)
    o_ref[...] = jnp.where(lane < hd, acc_ref[0], acc_ref[1]).astype(o_ref.dtype)


def sb_attention(qkv, batch, seq, d_model, tq=1024, tk=256):
    hd = d_model // ATT_HEADS
    npair = d_model // LANES
    tq = min(tq, seq)
    nq = seq // tq
    return pl.pallas_call(
        functools.partial(_sb_kernel, tq=tq, tk=tk, hd=hd),
        grid=(batch, npair, nq),
        in_specs=[pl.BlockSpec((tq, LANES), lambda b, p, i: (b * nq + i, p)),
                  pl.BlockSpec((seq, LANES), lambda b, p, i: (b, npair + p)),
                  pl.BlockSpec((seq, LANES), lambda b, p, i: (b, 2 * npair + p))],
        out_specs=pl.BlockSpec((tq, LANES), lambda b, p, i: (b * nq + i, p)),
        out_shape=jax.ShapeDtypeStruct((batch * seq, d_model), BF16),
        scratch_shapes=[pltpu.VMEM((2, tq, LANES), F32), pltpu.VMEM((2, tq, LANES), F32)],
        compiler_params=_params("parallel", "parallel", "arbitrary"),
        name="sb_attention",
    )(qkv, qkv, qkv)


def _fox_cum_kernel(f_ref, b_ref, o_ref, *, seq, blk):
    row = lax.broadcasted_iota(jnp.int32, (blk, blk), 0)
    col = lax.broadcasted_iota(jnp.int32, (blk, blk), 1)
    tri = jnp.where(col <= row, 1.0, 0.0).astype(BF16)
    run = jnp.zeros((1, f_ref.shape[-1]), F32)
    for c in range(seq // blk):
        lf = _log_sigmoid(f_ref[0, c * blk:(c + 1) * blk, :] + b_ref[...])
        hi, mid, lo = _split3(lf)
        cs = _dot(tri, hi) + _dot(tri, mid) + _dot(tri, lo) + run
        o_ref[0, c * blk:(c + 1) * blk, :] = cs
        run = cs[blk - 1:blk, :]


def fox_cum(f_pre, b_f, blk=256):
    batch, seq, nh = f_pre.shape
    blk = min(blk, seq)
    return pl.pallas_call(
        functools.partial(_fox_cum_kernel, seq=seq, blk=blk),
        grid=(batch,),
        in_specs=[pl.BlockSpec((1, seq, nh), lambda b: (b, 0, 0)),
                  pl.BlockSpec((1, nh), lambda b: (0, 0))],
        out_specs=pl.BlockSpec((1, seq, nh), lambda b: (b, 0, 0)),
        out_shape=jax.ShapeDtypeStruct((batch, seq, nh), F32),
        compiler_params=_params("parallel"),
        name="fox_cum",
    )(f_pre, b_f)


def _fox_kernel(q_ref, k_ref, v_ref, cr_ref, o_ref, acc_ref, m_ref, *, tq, tk, hd):
    i = pl.program_id(2)
    lane, q_heads = _head_queries(q_ref, tq, hd)
    klane = lax.broadcasted_iota(jnp.int32, (tk, LANES), 1)
    in_head = (klane < hd, klane >= hd)
    acc_ref[...] = jnp.zeros_like(acc_ref)
    m_ref[...] = jnp.full_like(m_ref, -jnp.inf)

    def scores(start, lo):
        ks = k_ref[pl.ds(start, tk), :]
        return [_dot_nt(q_heads[hh][lo:], ks) for hh in range(2)]

    def finish(s, start, lo, mask):
        vs = v_ref[pl.ds(start, tk), :]
        for hh in range(2):
            vh = jnp.where(in_head[hh], vs, jnp.ones_like(vs))
            cum_row = cr_ref[0, 0, hh:hh + 1, pl.ds(start, tk)]
            z = s[hh] - cum_row
            if mask is not None:
                z = jnp.where(mask, z, -jnp.inf)
            m_prev = m_ref[hh, lo:, :]
            m_new = jnp.maximum(m_prev, jnp.max(z, axis=-1, keepdims=True))
            a = jnp.exp(m_prev - m_new)
            p = jnp.exp(z - jnp.concatenate([m_new] * (tk // LANES), axis=1)).astype(BF16)
            m_ref[hh, lo:, :] = m_new
            acc_ref[hh, lo:, :] = a * acc_ref[hh, lo:, :] + _dot(p, vh)

    _sweep(i, tq, tk, scores, finish, _diag_masks(tq, tk, inclusive=True), ascending=True)
    acc0 = acc_ref[0]
    acc1 = acc_ref[1]
    o = jnp.where(lane < hd, acc0 / acc0[:, hd:hd + 1], acc1 / acc1[:, 0:1])
    o_ref[...] = o.astype(o_ref.dtype)


def fox_attention(qkv, cum, batch, seq, d_model, tq=1024, tk=256):
    hd = d_model // ATT_HEADS
    npair = d_model // LANES
    tq = min(tq, seq)
    nq = seq // tq
    cum_row = cum.reshape(batch, seq, npair, 2).transpose(0, 2, 3, 1)
    return pl.pallas_call(
        functools.partial(_fox_kernel, tq=tq, tk=tk, hd=hd),
        grid=(batch, npair, nq),
        in_specs=[pl.BlockSpec((tq, LANES), lambda b, p, i: (b * nq + i, p)),
                  pl.BlockSpec((seq, LANES), lambda b, p, i: (b, npair + p)),
                  pl.BlockSpec((seq, LANES), lambda b, p, i: (b, 2 * npair + p)),
                  pl.BlockSpec((1, 1, 2, seq), lambda b, p, i: (b, p, 0, 0))],
        out_specs=pl.BlockSpec((tq, LANES), lambda b, p, i: (b * nq + i, p)),
        out_shape=jax.ShapeDtypeStruct((batch * seq, d_model), BF16),
        scratch_shapes=[pltpu.VMEM((2, tq, LANES), F32), pltpu.VMEM((2, tq, LANES), F32)],
        compiler_params=_params("parallel", "parallel", "arbitrary"),
        name="fox_attention",
    )(qkv, qkv, qkv, cum_row)


def _lanes(x, width):
    return x if width == LANES else jnp.concatenate([x] * (width // LANES), axis=1)


def _mlstm_kernel(q_ref, k_ref, v_ref, og_ref, gc_ref, gr_ref, o_ref, c_ref, m_ref,
                  *, nh, dk, dv, L):
    @pl.when(pl.program_id(1) == 0)
    def _():
        c_ref[...] = jnp.zeros_like(c_ref)
        m_ref[...] = jnp.zeros_like(m_ref)

    row = lax.broadcasted_iota(jnp.int32, (L, L), 0)
    col = lax.broadcasted_iota(jnp.int32, (L, L), 1)
    causal = col <= row
    scale = dk ** -0.5
    ones = jnp.ones((L, LANES), BF16)
    bb = q_ref.shape[0]
    chains = [(bi, h) for bi in range(bb) for h in range(nh)]
    n_chains = len(chains)
    q = [q_ref[bi, :, h * dk:(h + 1) * dk] for bi, h in chains]
    k = [k_ref[bi, :, h * dk:(h + 1) * dk] for bi, h in chains]
    v = [jnp.concatenate([v_ref[bi, :, h * dv:(h + 1) * dv], ones], axis=1) for bi, h in chains]

    qk_raw = [_dot_nt(q[s], k[s]) for s in range(n_chains)]
    q_state = [_dot(q[s], c_ref[s].astype(BF16)) for s in range(n_chains)]

    w_intra, w_inter, floor, m_new, decay, kw = [], [], [], [], [], []
    for s, (bi, h) in enumerate(chains):
        gcol = gc_ref[bi]
        grow = gr_ref[bi, 0]
        i_col = gcol[:, h:h + 1]
        i_row = grow[h:h + 1, :]
        lf_col = _log_sigmoid(gcol[:, nh + h:nh + h + 1])
        lf_row = _log_sigmoid(grow[nh + h:nh + h + 1, :])
        hi, lo = _split2(jnp.where(causal, lf_row, 0.0))
        bcum = _dot(hi, ones) + _dot(lo, ones)
        bcum_row = jnp.sum(jnp.where(row <= col, lf_col, 0.0), axis=0, keepdims=True)
        m_prev = m_ref[s]
        d = jnp.where(causal, bcum[:, :L] - bcum_row + i_row, -jnp.inf)
        inter = bcum + m_prev
        mt = jnp.maximum(inter, jnp.max(d, axis=1, keepdims=True))
        mn = mt[L - 1:L, 0:1]
        blast = bcum[L - 1:L, 0:1]
        m_new.append(mn)
        w_intra.append(jnp.exp(d - mt[:, :L]))
        w_inter.append(jnp.exp(inter - mt))
        floor.append(jnp.exp(-mt))
        decay.append(jnp.exp(blast + m_prev - mn))
        kw.append((k[s].astype(F32) * scale) * jnp.exp(blast - _lanes(bcum, dk) + i_col - mn))

    qk = [(qk_raw[s] * (scale * w_intra[s])).astype(BF16) for s in range(n_chains)]
    intra = [_dot(qk[s], v[s]) for s in range(n_chains)]
    kv = [_dot_tn(kw[s].astype(BF16), v[s]) for s in range(n_chains)]

    new_c, outs = [], []
    for s, (bi, h) in enumerate(chains):
        numden = _lanes(w_inter[s], dv + LANES) * q_state[s] + intra[s]
        den = jnp.maximum(jnp.abs(numden[:, dv:]), floor[s])
        h_out = numden[:, :dv] / _lanes(den, dv)
        new_c.append(decay[s] * c_ref[s] + kv[s])
        gate = _sigmoid(og_ref[bi, :, h * dv:(h + 1) * dv].astype(F32))
        outs.append((h_out * gate).astype(o_ref.dtype))
    c_ref[...] = jnp.stack(new_c)
    m_ref[...] = jnp.stack(m_new)
    o_ref[...] = jnp.stack([jnp.concatenate(outs[bi * nh:(bi + 1) * nh], axis=1) for bi in range(bb)])


def mlstm_core(proj, gates, batch, seq, d_model, bb):
    nh, L = ML_HEADS, CHUNK
    dk = d_model // 2 // nh
    dv = d_model // nh
    nc = seq // L
    qw = nh * dk
    bb = min(bb, batch)
    proj = proj.reshape(batch, seq, 3 * d_model)
    gates_row = gates.reshape(batch, nc, L, 2 * nh).transpose(0, 1, 3, 2)
    out = pl.pallas_call(
        functools.partial(_mlstm_kernel, nh=nh, dk=dk, dv=dv, L=L),
        grid=(batch // bb, nc),
        in_specs=[pl.BlockSpec((bb, L, qw), lambda g, c: (g, c, 0)),
                  pl.BlockSpec((bb, L, qw), lambda g, c: (g, c, 1)),
                  pl.BlockSpec((bb, L, d_model), lambda g, c: (g, c, 1)),
                  pl.BlockSpec((bb, L, d_model), lambda g, c: (g, c, 2)),
                  pl.BlockSpec((bb, L, 2 * nh), lambda g, c: (g, c, 0)),
                  pl.BlockSpec((bb, 1, 2 * nh, L), lambda g, c: (g, c, 0, 0))],
        out_specs=pl.BlockSpec((bb, L, d_model), lambda g, c: (g, c, 0)),
        out_shape=jax.ShapeDtypeStruct((batch, seq, d_model), BF16),
        scratch_shapes=[pltpu.VMEM((bb * nh, dk, dv + LANES), F32),
                        pltpu.VMEM((bb * nh, 1, 1), F32)],
        compiler_params=_params("parallel", "arbitrary"),
        name="mlstm_core",
    )(proj, proj, proj, proj, gates, gates_row)
    return out.reshape(batch * seq, d_model)


def _gla_kernel(q_ref, k_ref, v_ref, r_ref, a_ref, wh_ref, wl_ref, ba_ref, ng_ref, o_ref, st_ref,
                *, nh, dk, dv, L):
    @pl.when(pl.program_id(1) == 0)
    def _():
        st_ref[...] = jnp.zeros_like(st_ref)

    row = lax.broadcasted_iota(jnp.int32, (L, L), 0)
    col = lax.broadcasted_iota(jnp.int32, (L, L), 1)
    causal = col <= row
    tri = jnp.where(causal, 1.0, 0.0).astype(BF16)
    scale = dk ** -0.5
    bb = q_ref.shape[0]
    chains = [(bi, h) for bi in range(bb) for h in range(nh)]

    logits = []
    for bi in range(bb):
        a_hi, a_lo = _split2(a_ref[bi])
        wh = wh_ref[...]
        logits.append(_dot(a_hi, wh) + _dot(a_lo, wh) + _dot(a_hi, wl_ref[...]) + ba_ref[...])

    bcums = []
    for bi in range(bb):
        la_hi, la_lo = _split2(_log_sigmoid(logits[bi]) * (1.0 / GLA_TAU))
        bcums.append(_dot(tri, la_hi) + _dot(tri, la_lo))

    v, q_rel, k_rel, q_dec, k_dec, carry = [], [], [], [], [], []
    for bi, h in chains:
        hs = slice(h * dk, (h + 1) * dk)
        bcum = bcums[bi][:, hs]
        blast = bcum[L - 1:L, :]
        ref = bcum[L // 2 - 1:L // 2, :]
        qf = q_ref[bi, :, hs].astype(F32) * scale
        kf = k_ref[bi, :, hs].astype(F32)
        v.append(v_ref[bi, :, h * dv:(h + 1) * dv])
        q_rel.append((qf * jnp.exp(bcum - ref)).astype(BF16))
        k_rel.append((kf * jnp.exp(ref - bcum)).astype(BF16))
        q_dec.append((qf * jnp.exp(bcum)).astype(BF16))
        k_dec.append((kf * jnp.exp(blast - bcum)).astype(BF16))
        carry.append(jnp.exp(blast))

    n = len(chains)
    att = [jnp.where(causal, _dot_nt(q_rel[s], k_rel[s]), 0.0).astype(BF16) for s in range(n)]
    o_state = [_dot_nt(q_dec[s], st_ref[s].astype(BF16)) for s in range(n)]
    kv = [_dot_tn(v[s], k_dec[s]) for s in range(n)]

    o_intra = [_dot(att[s], v[s]) for s in range(n)]
    new_st, outs = [], []
    for s, (bi, h) in enumerate(chains):
        o = o_intra[s] + o_state[s]
        o = o * lax.rsqrt(jnp.mean(o * o, axis=-1, keepdims=True) + RMS_EPS) * ng_ref[...]
        r = r_ref[bi, :, h * dv:(h + 1) * dv].astype(F32)
        outs.append((o * (r * _sigmoid(r))).astype(o_ref.dtype))
        new_st.append(st_ref[s] * carry[s] + kv[s])
    st_ref[...] = jnp.stack(new_st)
    o_ref[...] = jnp.stack([jnp.concatenate(outs[bi * nh:(bi + 1) * nh], axis=1) for bi in range(bb)])


def gla_core(proj, a_low, wa_hi, wa_lo, b_a, norm_g, batch, seq, d_model, bb):
    nh, L = GLA_HEADS, CHUNK
    dk = d_model // 2 // nh
    dv = d_model // nh
    nc = seq // L
    qw = nh * dk
    bb = min(bb, batch)
    proj = proj.reshape(batch, seq, 3 * d_model)
    a_low = a_low.reshape(batch, seq, LANES)
    out = pl.pallas_call(
        functools.partial(_gla_kernel, nh=nh, dk=dk, dv=dv, L=L),
        grid=(batch // bb, nc),
        in_specs=[pl.BlockSpec((bb, L, qw), lambda g, c: (g, c, 0)),
                  pl.BlockSpec((bb, L, qw), lambda g, c: (g, c, 1)),
                  pl.BlockSpec((bb, L, d_model), lambda g, c: (g, c, 1)),
                  pl.BlockSpec((bb, L, d_model), lambda g, c: (g, c, 2)),
                  pl.BlockSpec((bb, L, LANES), lambda g, c: (g, c, 0)),
                  pl.BlockSpec((LANES, qw), lambda g, c: (0, 0)),
                  pl.BlockSpec((LANES, qw), lambda g, c: (0, 0)),
                  pl.BlockSpec((1, qw), lambda g, c: (0, 0)),
                  pl.BlockSpec((1, dv), lambda g, c: (0, 0))],
        out_specs=pl.BlockSpec((bb, L, d_model), lambda g, c: (g, c, 0)),
        out_shape=jax.ShapeDtypeStruct((batch, seq, d_model), BF16),
        scratch_shapes=[pltpu.VMEM((bb * nh, dv, dk), F32)],
        compiler_params=_params("parallel", "arbitrary"),
        name="gla_core",
    )(proj, proj, proj, proj, a_low, wa_hi, wa_lo, b_a, norm_g)
    return out.reshape(batch * seq, d_model)


def _mix_xattn_kernel(y_ref, hf_ref, wm_ref, g0_ref, b0_ref, wq_ref, kv_ref, wo_ref, g1_ref, b1_ref,
                      of_ref, ob_ref, *, nh, alpha):
    d = wq_ref.shape[1]
    hd = d // nh
    h1 = _layer_norm(alpha * hf_ref[...] + _dot(y_ref[...], wm_ref[...]), g0_ref[...], b0_ref[...])
    q = (_dot(h1.astype(BF16), wq_ref[...]) * (hd ** -0.5)).astype(BF16)
    outs = []
    for h in range(nh):
        kh = kv_ref[:, h * hd:(h + 1) * hd]
        vh = kv_ref[:, d + h * hd:d + (h + 1) * hd]
        s = _dot_nt(q[:, h * hd:(h + 1) * hd], kh)
        p = jnp.exp(s - jnp.max(s, axis=-1, keepdims=True))
        l = jnp.sum(p, axis=-1, keepdims=True)
        outs.append((_dot(p.astype(BF16), vh) / l).astype(BF16))
    o = jnp.concatenate(outs, axis=-1)
    h2 = _layer_norm(alpha * h1 + _dot(o, wo_ref[...]), g1_ref[...], b1_ref[...])
    of_ref[...] = h2
    ob_ref[...] = h2.astype(BF16)


def mix_xattn_block(y, hf, w_mix, g0, b0, wq, kv, wo, g1, b1, batch, seq, n_mem, alpha, tm):
    d = hf.shape[1]
    tm = min(tm, seq)
    nt = seq // tm
    rows = pl.BlockSpec((tm, d), lambda bi, i: (bi * nt + i, 0))
    whole = lambda shape: pl.BlockSpec(shape, lambda bi, i: (0, 0), pipeline_mode=pl.Buffered(1))
    return pl.pallas_call(
        functools.partial(_mix_xattn_kernel, nh=XA_HEADS, alpha=alpha),
        grid=(batch, nt),
        in_specs=[rows, rows, whole((d, d)), whole((1, d)), whole((1, d)),
                  whole((d, d)), pl.BlockSpec((n_mem, 2 * d), lambda bi, i: (bi, 0)), whole((d, d)),
                  whole((1, d)), whole((1, d))],
        out_specs=[rows, rows],
        out_shape=[jax.ShapeDtypeStruct((batch * seq, d), F32),
                   jax.ShapeDtypeStruct((batch * seq, d), BF16)],
        compiler_params=_params("parallel", "parallel"),
        name="mix_xattn_block",
    )(y, hf, w_mix, g0, b0, wq, kv, wo, g1, b1)


def _gelu_tanh(x):
    return 0.5 * x * (1.0 + jnp.tanh(0.7978845608028654 * (x + 0.044715 * (x * x * x))))


def _ffn_kernel(hb_ref, hf_ref, wu_ref, cw_ref, cb_ref, wd_ref, g_ref, b_ref, of_ref, ob_ref, tail_ref,
                *, tm, tf, group, blocks_per_seq, alpha):
    i = pl.program_id(0)
    f = wd_ref.shape[0]

    @pl.when(i % blocks_per_seq == 0)
    def _():
        tail_ref[...] = jnp.zeros_like(tail_ref)

    hb = hb_ref[...]
    row = lax.broadcasted_iota(jnp.int32, (SUBLANES, tf), 0)

    def up(c):
        return [_dot(hb, wu_ref[:, lo:lo + tf]) for lo in (c * tf, f + c * tf)]

    def conv(u, lo):
        cols = slice(lo, lo + tf)
        tail = tail_ref[:, cols]
        tail_ref[:, cols] = u[tm - SUBLANES:, :]
        p1 = tail[SUBLANES - 1:SUBLANES, :]
        p2 = tail[SUBLANES - 2:SUBLANES - 1, :]
        r1 = pltpu.roll(u, 1, 0)
        r2 = pltpu.roll(u, 2, 0)
        top1 = jnp.where(row == 0, p1, r1[:SUBLANES])
        top2 = jnp.where(row == 0, p2, jnp.where(row == 1, p1, r2[:SUBLANES]))
        u1 = jnp.concatenate([top1, r1[SUBLANES:]], axis=0)
        u2 = jnp.concatenate([top2, r2[SUBLANES:]], axis=0)
        c = cw_ref[:, cols]
        return c[0:1, :] * u2 + c[1:2, :] * u1 + c[2:3, :] * u + cb_ref[:, cols]

    n = f // tf
    y = alpha * hf_ref[...]
    u_next = up(0)
    acts, first = [], 0
    for c in range(n):
        u_gate, u_val = u_next
        if c + 1 < n:
            u_next = up(c + 1)
        acts.append((_gelu_tanh(conv(u_gate, c * tf)) * conv(u_val, f + c * tf)).astype(BF16))
        if len(acts) == group or c + 1 == n:
            y = y + _dot(jnp.concatenate(acts, axis=1), wd_ref[first * tf:(c + 1) * tf, :])
            acts, first = [], c + 1
    y = _layer_norm(y, g_ref[...], b_ref[...])
    of_ref[...] = y
    ob_ref[...] = y.astype(BF16)


def ffn_block(hb, hf, w_up, conv_w, conv_b, w_down, g, b, seq, alpha, tm, tf, group):
    t, d = hb.shape
    f = w_down.shape[0]
    tm = min(tm, seq)
    whole = lambda shape: pl.BlockSpec(shape, lambda i: (0, 0), pipeline_mode=pl.Buffered(1))
    return pl.pallas_call(
        functools.partial(_ffn_kernel, tm=tm, tf=tf, group=group, blocks_per_seq=seq // tm, alpha=alpha),
        grid=(t // tm,),
        in_specs=[pl.BlockSpec((tm, d), lambda i: (i, 0)),
                  pl.BlockSpec((tm, d), lambda i: (i, 0)),
                  whole((d, 2 * f)),
                  whole((CONV_W, 2 * f)),
                  whole((1, 2 * f)),
                  whole((f, d)),
                  whole((1, d)),
                  whole((1, d))],
        out_specs=[pl.BlockSpec((tm, d), lambda i: (i, 0)),
                   pl.BlockSpec((tm, d), lambda i: (i, 0))],
        out_shape=[jax.ShapeDtypeStruct((t, d), F32),
                   jax.ShapeDtypeStruct((t, d), BF16)],
        scratch_shapes=[pltpu.VMEM((SUBLANES, 2 * f), F32)],
        compiler_params=_params("arbitrary"),
        name="ffn_block",
    )(hb, hf, w_up, conv_w, conv_b, w_down, g, b)


def _pad_cols(w, width):
    return jnp.pad(w, ((0, 0), (0, width - w.shape[1])))


def kernel(x, mem, ln_g, ln_b, mix_wo, sb_win, fox_win, fox_bf, ml_win, ml_bi, ml_bf,
           gla_win, gla_wa2, gla_ba, gla_norm_g, xa_wq, xa_wkv, xa_wo,
           ffn_up, ffn_conv, ffn_conv_b, ffn_down):
    batch, seq, d = x.shape
    n_mem = mem.shape[1]
    depth = mix_wo.shape[0]
    t = batch * seq
    alpha = (2.0 * depth) ** 0.25
    main = 3 * d

    hf = x.reshape(t, d)
    hb = hf
    memb = mem.reshape(batch * n_mem, d).astype(BF16)

    for layer in range(depth):
        kind = layer % 4
        occ = layer // 4
        w_in = (sb_win, fox_win, ml_win, gla_win)[kind][occ]
        w_main = w_in[:, :main].astype(BF16)
        if kind == 0:
            proj = linear(hb, w_main, BF16, tm=1024, tn=1024)
            y = sb_attention(proj, batch, seq, d)
        else:
            proj, narrow = in_proj(hb, w_main, _pad_cols(w_in[:, main:], LANES).astype(BF16), tm=1024, tn=1024)
        if kind == 1:
            cum = fox_cum(narrow.reshape(batch, seq, LANES), _pad_cols(fox_bf[occ][None, :], LANES))
            y = fox_attention(proj, cum[:, :, :ATT_HEADS], batch, seq, d)
        elif kind == 2:
            gates = narrow[:, :2 * ML_HEADS] + jnp.concatenate([ml_bi[occ], ml_bf[occ]])[None, :]
            y = mlstm_core(proj, gates.reshape(batch, seq, 2 * ML_HEADS), batch, seq, d, bb=4)
        elif kind == 3:
            a_low = narrow
            wa = jnp.pad(gla_wa2[occ], ((0, LANES - GLA_RANK), (0, 0)))
            wa_hi = wa.astype(BF16)
            wa_lo = (wa - wa_hi.astype(F32)).astype(BF16)
            y = gla_core(proj, a_low, wa_hi, wa_lo, gla_ba[occ][None, :], gla_norm_g[occ][None, :],
                         batch, seq, d, bb=4)
        kv = linear(memb, xa_wkv[layer].astype(BF16), BF16, tm=1024, tn=1024)
        hf, hb = mix_xattn_block(y, hf, mix_wo[layer].astype(BF16), ln_g[layer, 0][None, :], ln_b[layer, 0][None, :],
                                 xa_wq[layer].astype(BF16), kv, xa_wo[layer].astype(BF16),
                                 ln_g[layer, 1][None, :], ln_b[layer, 1][None, :],
                                 batch, seq, n_mem, alpha, tm=512)

        hf, hb = ffn_block(hb, hf, ffn_up[layer].astype(BF16), ffn_conv[layer, :, 0, :],
                           ffn_conv_b[layer][None, :], ffn_down[layer].astype(BF16),
                           ln_g[layer, 2][None, :], ln_b[layer, 2][None, :],
                           seq, alpha, tm=512, tf=256, group=4)
    return hf.reshape(batch, seq, d)
```

```python
import functools

import jax
import jax.numpy as jnp
from jax import lax
from jax.experimental import pallas as pl
from jax.experimental.pallas import tpu as pltpu

F32 = jnp.float32
BF16 = jnp.bfloat16

ATT_HEADS = 16
ML_HEADS = 4
GLA_HEADS = 4
GLA_RANK = 16
GLA_TAU = 16.0
XA_HEADS = 4
CHUNK = 64
CONV_W = 3
LN_EPS = 1e-5
RMS_EPS = 1e-6

LANES = 128
SUBLANES = 8
VMEM_LIMIT = 48 * 1024 * 1024

_NT = (((1,), (1,)), ((), ()))
_TN = (((0,), (0,)), ((), ()))


def _params(*sem):
    return pltpu.CompilerParams(dimension_semantics=sem, vmem_limit_bytes=VMEM_LIMIT)


def _dot(a, b):
    return jnp.dot(a, b, preferred_element_type=F32)


def _dot_nt(a, b):
    return lax.dot_general(a, b, _NT, preferred_element_type=F32)


def _dot_tn(a, b):
    return lax.dot_general(a, b, _TN, preferred_element_type=F32)


def _log_sigmoid(x):
    return jnp.minimum(x, 0.0) - jnp.log1p(jnp.exp(-jnp.abs(x)))


def _sigmoid(x):
    return 1.0 / (1.0 + jnp.exp(-x))


def _split2(x):
    hi = x.astype(BF16)
    lo = (x - hi.astype(F32)).astype(BF16)
    return hi, lo


def _split3(x):
    hi = x.astype(BF16)
    r = x - hi.astype(F32)
    mid = r.astype(BF16)
    lo = (r - mid.astype(F32)).astype(BF16)
    return hi, mid, lo


def _layer_norm(z, g, b):
    mu = jnp.mean(z, axis=-1, keepdims=True)
    zc = z - mu
    var = jnp.mean(zc * zc, axis=-1, keepdims=True)
    return zc * lax.rsqrt(var + LN_EPS) * g + b


def _linear_kernel(x_ref, w_ref, o_ref):
    o_ref[...] = _dot(x_ref[...].astype(BF16), w_ref[...]).astype(o_ref.dtype)


def linear(x, w, layer, n, out_dtype, tm, tn):
    m, k = x.shape
    tm = min(tm, m)
    tn = min(tn, n)
    return pl.pallas_call(
        _linear_kernel,
        grid=(m // tm, n // tn),
        in_specs=[pl.BlockSpec((tm, k), lambda i, j: (i, 0)),
                  pl.BlockSpec((None, k, tn), lambda i, j: (layer, 0, j))],
        out_specs=pl.BlockSpec((tm, tn), lambda i, j: (i, j)),
        out_shape=jax.ShapeDtypeStruct((m, n), out_dtype),
        compiler_params=_params("parallel", "arbitrary"),
        name="linear",
    )(x, w)


def _in_proj_kernel(x_ref, w_ref, wg_ref, o_ref, og_ref):
    x = x_ref[...].astype(BF16)
    o_ref[...] = _dot(x, w_ref[...]).astype(o_ref.dtype)

    @pl.when(pl.program_id(1) == 0)
    def _():
        og_ref[...] = _dot(x, wg_ref[...])


def in_proj(x, w, layer, n, wg, tm, tn):
    m, k = x.shape
    tm = min(tm, m)
    tn = min(tn, n)
    return pl.pallas_call(
        _in_proj_kernel,
        grid=(m // tm, n // tn),
        in_specs=[pl.BlockSpec((tm, k), lambda i, j: (i, 0)),
                  pl.BlockSpec((None, k, tn), lambda i, j: (layer, 0, j)),
                  pl.BlockSpec((k, LANES), lambda i, j: (0, 0))],
        out_specs=[pl.BlockSpec((tm, tn), lambda i, j: (i, j)),
                   pl.BlockSpec((tm, LANES), lambda i, j: (i, 0))],
        out_shape=[jax.ShapeDtypeStruct((m, n), BF16),
                   jax.ShapeDtypeStruct((m, LANES), F32)],
        compiler_params=_params("parallel", "arbitrary"),
        name="in_proj",
    )(x, w, wg)


def _head_queries(q_ref, tq, hd):
    lane = lax.broadcasted_iota(jnp.int32, (tq, LANES), 1)
    qs = q_ref[...] * (hd ** -0.5)
    zero = jnp.zeros_like(qs)
    return lane, (jnp.where(lane < hd, qs, zero), jnp.where(lane >= hd, qs, zero))


def _sweep(i, tq, tk, scores, finish, masks, ascending):
    nd = tq // tk
    nfull = i * nd
    per_trip = 2 if nd % 2 == 0 else 1

    def diagonal():
        for c in (range(nd) if ascending else reversed(range(nd))):
            start = pl.multiple_of(i * tq + c * tk, tk)
            finish(scores(start, c * tk), start, c * tk, masks[c])

    def full_body(j, carry):
        blocks = [per_trip * j + u if ascending else nfull - 1 - per_trip * j - u for u in range(per_trip)]
        starts = [pl.multiple_of(kb * tk, tk) for kb in blocks]
        z = [scores(start, 0) for start in starts]
        for start, zz in zip(starts, z):
            finish(zz, start, 0, None)
        return carry

    if ascending:
        lax.fori_loop(0, nfull // per_trip, full_body, 0)
        diagonal()
    else:
        diagonal()
        lax.fori_loop(0, nfull // per_trip, full_body, 0)


def _diag_masks(tq, tk, inclusive):
    masks = []
    for c in range(tq // tk):
        n = tq - c * tk
        col = lax.broadcasted_iota(jnp.int32, (n, tk), 1)
        row = lax.broadcasted_iota(jnp.int32, (n, tk), 0)
        masks.append(col <= row if inclusive else col < row)
    return masks


def _sb_kernel(q_ref, k_ref, v_ref, o_ref, acc_ref, run_ref, *, tq, tk, hd):
    i = pl.program_id(2)
    lane, q_heads = _head_queries(q_ref, tq, hd)
    r = lax.broadcasted_iota(jnp.int32, (tk, tk), 0)
    c = lax.broadcasted_iota(jnp.int32, (tk, tk), 1)
    neg_later = jnp.where(r > c, -1.0, 0.0).astype(BF16)
    acc_ref[...] = jnp.zeros_like(acc_ref)
    run_ref[...] = jnp.zeros_like(run_ref)

    def scores(start, lo):
        ks = k_ref[pl.ds(start, tk), :]
        return [_dot_nt(q_heads[hh][lo:], ks) for hh in range(2)]

    def finish(z, start, lo, mask):
        vs = v_ref[pl.ds(start, tk), :]
        for hh in range(2):
            zb = z[hh].astype(BF16)
            soft = jnp.log(1.0 + jnp.exp(-jnp.abs(zb)))
            drop = jnp.maximum(zb, 0.0) + soft
            if mask is not None:
                drop = jnp.where(mask, drop, jnp.zeros_like(drop))
            suffix = _dot(drop, neg_later)
            total = suffix[:, 0:1] - drop[:, 0:1].astype(F32)
            run = run_ref[hh, lo:, :]
            log_sig = jnp.minimum(z[hh], 0.0) - soft.astype(F32)
            w = jnp.exp(log_sig + suffix + jnp.concatenate([run] * (tk // LANES), axis=1))
            if mask is not None:
                w = jnp.where(mask, w, 0.0)
            acc_ref[hh, lo:, :] += _dot(w.astype(BF16), vs)
            run_ref[hh, lo:, :] = run + total

    _sweep(i, tq, tk, scores, finish, _diag_masks(tq, tk, inclusive=False), ascending=False)
    o_ref[...] = jnp.where(lane < hd, acc_ref[0], acc_ref[1]).astype(o_ref.dtype)


def sb_attention(qkv, batch, seq, d_model, tq=1024, tk=256):
    hd = d_model // ATT_HEADS
    npair = d_model // LANES
    tq = min(tq, seq)
    nq = seq // tq
    return pl.pallas_call(
        functools.partial(_sb_kernel, tq=tq, tk=tk, hd=hd),
        grid=(batch, npair, nq),
        in_specs=[pl.BlockSpec((tq, LANES), lambda b, p, i: (b * nq + i, p)),
                  pl.BlockSpec((seq, LANES), lambda b, p, i: (b, npair + p)),
                  pl.BlockSpec((seq, LANES), lambda b, p, i: (b, 2 * npair + p))],
        out_specs=pl.BlockSpec((tq, LANES), lambda b, p, i: (b * nq + i, p)),
        out_shape=jax.ShapeDtypeStruct((batch * seq, d_model), BF16),
        scratch_shapes=[pltpu.VMEM((2, tq, LANES), F32), pltpu.VMEM((2, tq, LANES), F32)],
        compiler_params=_params("parallel", "parallel", "arbitrary"),
        name="sb_attention",
    )(qkv, qkv, qkv)


def _fox_cum_kernel(f_ref, b_ref, o_ref, *, seq, blk):
    row = lax.broadcasted_iota(jnp.int32, (blk, blk), 0)
    col = lax.broadcasted_iota(jnp.int32, (blk, blk), 1)
    tri = jnp.where(col <= row, 1.0, 0.0).astype(BF16)
    run = jnp.zeros((1, f_ref.shape[-1]), F32)
    for c in range(seq // blk):
        lf = _log_sigmoid(f_ref[0, c * blk:(c + 1) * blk, :] + b_ref[...])
        hi, mid, lo = _split3(lf)
        cs = _dot(tri, hi) + _dot(tri, mid) + _dot(tri, lo) + run
        o_ref[0, c * blk:(c + 1) * blk, :] = cs
        run = cs[blk - 1:blk, :]


def fox_cum(f_pre, b_f, blk=256):
    batch, seq, nh = f_pre.shape
    blk = min(blk, seq)
    return pl.pallas_call(
        functools.partial(_fox_cum_kernel, seq=seq, blk=blk),
        grid=(batch,),
        in_specs=[pl.BlockSpec((1, seq, nh), lambda b: (b, 0, 0)),
                  pl.BlockSpec((1, nh), lambda b: (0, 0))],
        out_specs=pl.BlockSpec((1, seq, nh), lambda b: (b, 0, 0)),
        out_shape=jax.ShapeDtypeStruct((batch, seq, nh), F32),
        compiler_params=_params("parallel"),
        name="fox_cum",
    )(f_pre, b_f)


def _fox_kernel(q_ref, k_ref, v_ref, cr_ref, o_ref, acc_ref, m_ref, *, tq, tk, hd):
    i = pl.program_id(2)
    lane, q_heads = _head_queries(q_ref, tq, hd)
    klane = lax.broadcasted_iota(jnp.int32, (tk, LANES), 1)
    in_head = (klane < hd, klane >= hd)
    acc_ref[...] = jnp.zeros_like(acc_ref)
    m_ref[...] = jnp.full_like(m_ref, -jnp.inf)

    def scores(start, lo):
        ks = k_ref[pl.ds(start, tk), :]
        return [_dot_nt(q_heads[hh][lo:], ks) for hh in range(2)]

    def finish(s, start, lo, mask):
        vs = v_ref[pl.ds(start, tk), :]
        for hh in range(2):
            vh = jnp.where(in_head[hh], vs, jnp.ones_like(vs))
            cum_row = cr_ref[0, 0, hh:hh + 1, pl.ds(start, tk)]
            z = s[hh] - cum_row
            if mask is not None:
                z = jnp.where(mask, z, -jnp.inf)
            m_prev = m_ref[hh, lo:, :]
            m_new = jnp.maximum(m_prev, jnp.max(z, axis=-1, keepdims=True))
            a = jnp.exp(m_prev - m_new)
            p = jnp.exp(z - jnp.concatenate([m_new] * (tk // LANES), axis=1)).astype(BF16)
            m_ref[hh, lo:, :] = m_new
            acc_ref[hh, lo:, :] = a * acc_ref[hh, lo:, :] + _dot(p, vh)

    _sweep(i, tq, tk, scores, finish, _diag_masks(tq, tk, inclusive=True), ascending=True)
    acc0 = acc_ref[0]
    acc1 = acc_ref[1]
    o = jnp.where(lane < hd, acc0 / acc0[:, hd:hd + 1], acc1 / acc1[:, 0:1])
    o_ref[...] = o.astype(o_ref.dtype)


def fox_attention(qkv, cum, batch, seq, d_model, tq=1024, tk=256):
    hd = d_model // ATT_HEADS
    npair = d_model // LANES
    tq = min(tq, seq)
    nq = seq // tq
    cum_row = cum.reshape(batch, seq, npair, 2).transpose(0, 2, 3, 1)
    return pl.pallas_call(
        functools.partial(_fox_kernel, tq=tq, tk=tk, hd=hd),
        grid=(batch, npair, nq),
        in_specs=[pl.BlockSpec((tq, LANES), lambda b, p, i: (b * nq + i, p)),
                  pl.BlockSpec((seq, LANES), lambda b, p, i: (b, npair + p)),
                  pl.BlockSpec((seq, LANES), lambda b, p, i: (b, 2 * npair + p)),
                  pl.BlockSpec((1, 1, 2, seq), lambda b, p, i: (b, p, 0, 0))],
        out_specs=pl.BlockSpec((tq, LANES), lambda b, p, i: (b * nq + i, p)),
        out_shape=jax.ShapeDtypeStruct((batch * seq, d_model), BF16),
        scratch_shapes=[pltpu.VMEM((2, tq, LANES), F32), pltpu.VMEM((2, tq, LANES), F32)],
        compiler_params=_params("parallel", "parallel", "arbitrary"),
        name="fox_attention",
    )(qkv, qkv, qkv, cum_row)


def _lanes(x, width):
    return x if width == LANES else jnp.concatenate([x] * (width // LANES), axis=1)


def _mlstm_kernel(q_ref, k_ref, v_ref, og_ref, gc_ref, gr_ref, o_ref, c_ref, m_ref,
                  *, nh, dk, dv, L):
    @pl.when(pl.program_id(1) == 0)
    def _():
        c_ref[...] = jnp.zeros_like(c_ref)
        m_ref[...] = jnp.zeros_like(m_ref)

    row = lax.broadcasted_iota(jnp.int32, (L, L), 0)
    col = lax.broadcasted_iota(jnp.int32, (L, L), 1)
    causal = col <= row
    scale = dk ** -0.5
    ones = jnp.ones((L, LANES), BF16)
    bb = q_ref.shape[0]
    chains = [(bi, h) for bi in range(bb) for h in range(nh)]
    n_chains = len(chains)
    q = [q_ref[bi, :, h * dk:(h + 1) * dk] for bi, h in chains]
    k = [k_ref[bi, :, h * dk:(h + 1) * dk] for bi, h in chains]
    v = [jnp.concatenate([v_ref[bi, :, h * dv:(h + 1) * dv], ones], axis=1) for bi, h in chains]

    qk_raw = [_dot_nt(q[s], k[s]) for s in range(n_chains)]
    q_state = [_dot(q[s], c_ref[s].astype(BF16)) for s in range(n_chains)]

    w_intra, w_inter, floor, m_new, decay, kw = [], [], [], [], [], []
    for s, (bi, h) in enumerate(chains):
        gcol = gc_ref[bi]
        grow = gr_ref[bi, 0]
        i_col = gcol[:, h:h + 1]
        i_row = grow[h:h + 1, :]
        lf_col = _log_sigmoid(gcol[:, nh + h:nh + h + 1])
        lf_row = _log_sigmoid(grow[nh + h:nh + h + 1, :])
        hi, lo = _split2(jnp.where(causal, lf_row, 0.0))
        bcum = _dot(hi, ones) + _dot(lo, ones)
        bcum_row = jnp.sum(jnp.where(row <= col, lf_col, 0.0), axis=0, keepdims=True)
        m_prev = m_ref[s]
        d = jnp.where(causal, bcum[:, :L] - bcum_row + i_row, -jnp.inf)
        inter = bcum + m_prev
        mt = jnp.maximum(inter, jnp.max(d, axis=1, keepdims=True))
        mn = mt[L - 1:L, 0:1]
        blast = bcum[L - 1:L, 0:1]
        m_new.append(mn)
        w_intra.append(jnp.exp(d - mt[:, :L]))
        w_inter.append(jnp.exp(inter - mt))
        floor.append(jnp.exp(-mt))
        decay.append(jnp.exp(blast + m_prev - mn))
        kw.append((k[s].astype(F32) * scale) * jnp.exp(blast - _lanes(bcum, dk) + i_col - mn))

    qk = [(qk_raw[s] * (scale * w_intra[s])).astype(BF16) for s in range(n_chains)]
    intra = [_dot(qk[s], v[s]) for s in range(n_chains)]
    kv = [_dot_tn(kw[s].astype(BF16), v[s]) for s in range(n_chains)]

    new_c, outs = [], []
    for s, (bi, h) in enumerate(chains):
        numden = _lanes(w_inter[s], dv + LANES) * q_state[s] + intra[s]
        den = jnp.maximum(jnp.abs(numden[:, dv:]), floor[s])
        h_out = numden[:, :dv] / _lanes(den, dv)
        new_c.append(decay[s] * c_ref[s] + kv[s])
        gate = _sigmoid(og_ref[bi, :, h * dv:(h + 1) * dv].astype(F32))
        outs.append((h_out * gate).astype(o_ref.dtype))
    c_ref[...] = jnp.stack(new_c)
    m_ref[...] = jnp.stack(m_new)
    o_ref[...] = jnp.stack([jnp.concatenate(outs[bi * nh:(bi + 1) * nh], axis=1) for bi in range(bb)])


def mlstm_core(proj, gates, batch, seq, d_model, bb):
    nh, L = ML_HEADS, CHUNK
    dk = d_model // 2 // nh
    dv = d_model // nh
    nc = seq // L
    qw = nh * dk
    bb = min(bb, batch)
    proj = proj.reshape(batch, seq, 3 * d_model)
    gates_row = gates.reshape(batch, nc, L, 2 * nh).transpose(0, 1, 3, 2)
    out = pl.pallas_call(
        functools.partial(_mlstm_kernel, nh=nh, dk=dk, dv=dv, L=L),
        grid=(batch // bb, nc),
        in_specs=[pl.BlockSpec((bb, L, qw), lambda g, c: (g, c, 0)),
                  pl.BlockSpec((bb, L, qw), lambda g, c: (g, c, 1)),
                  pl.BlockSpec((bb, L, d_model), lambda g, c: (g, c, 1)),
                  pl.BlockSpec((bb, L, d_model), lambda g, c: (g, c, 2)),
                  pl.BlockSpec((bb, L, 2 * nh), lambda g, c: (g, c, 0)),
                  pl.BlockSpec((bb, 1, 2 * nh, L), lambda g, c: (g, c, 0, 0))],
        out_specs=pl.BlockSpec((bb, L, d_model), lambda g, c: (g, c, 0)),
        out_shape=jax.ShapeDtypeStruct((batch, seq, d_model), BF16),
        scratch_shapes=[pltpu.VMEM((bb * nh, dk, dv + LANES), F32),
                        pltpu.VMEM((bb * nh, 1, 1), F32)],
        compiler_params=_params("parallel", "arbitrary"),
        name="mlstm_core",
    )(proj, proj, proj, proj, gates, gates_row)
    return out.reshape(batch * seq, d_model)


def _gla_kernel(q_ref, k_ref, v_ref, r_ref, a_ref, wh_ref, wl_ref, ba_ref, ng_ref, o_ref, st_ref,
                *, nh, dk, dv, L):
    @pl.when(pl.program_id(1) == 0)
    def _():
        st_ref[...] = jnp.zeros_like(st_ref)

    row = lax.broadcasted_iota(jnp.int32, (L, L), 0)
    col = lax.broadcasted_iota(jnp.int32, (L, L), 1)
    causal = col <= row
    tri = jnp.where(causal, 1.0, 0.0).astype(BF16)
    scale = dk ** -0.5
    bb = q_ref.shape[0]
    chains = [(bi, h) for bi in range(bb) for h in range(nh)]

    logits = []
    for bi in range(bb):
        a_hi, a_lo = _split2(a_ref[bi])
        wh = wh_ref[...]
        logits.append(_dot(a_hi, wh) + _dot(a_lo, wh) + _dot(a_hi, wl_ref[...]) + ba_ref[...])

    bcums = []
    for bi in range(bb):
        la_hi, la_lo = _split2(_log_sigmoid(logits[bi]) * (1.0 / GLA_TAU))
        bcums.append(_dot(tri, la_hi) + _dot(tri, la_lo))

    v, q_rel, k_rel, q_dec, k_dec, carry = [], [], [], [], [], []
    for bi, h in chains:
        hs = slice(h * dk, (h + 1) * dk)
        bcum = bcums[bi][:, hs]
        blast = bcum[L - 1:L, :]
        ref = bcum[L // 2 - 1:L // 2, :]
        qf = q_ref[bi, :, hs].astype(F32) * scale
        kf = k_ref[bi, :, hs].astype(F32)
        v.append(v_ref[bi, :, h * dv:(h + 1) * dv])
        q_rel.append((qf * jnp.exp(bcum - ref)).astype(BF16))
        k_rel.append((kf * jnp.exp(ref - bcum)).astype(BF16))
        q_dec.append((qf * jnp.exp(bcum)).astype(BF16))
        k_dec.append((kf * jnp.exp(blast - bcum)).astype(BF16))
        carry.append(jnp.exp(blast))

    n = len(chains)
    att = [jnp.where(causal, _dot_nt(q_rel[s], k_rel[s]), 0.0).astype(BF16) for s in range(n)]
    o_state = [_dot_nt(q_dec[s], st_ref[s].astype(BF16)) for s in range(n)]
    kv = [_dot_tn(v[s], k_dec[s]) for s in range(n)]

    o_intra = [_dot(att[s], v[s]) for s in range(n)]
    new_st, outs = [], []
    for s, (bi, h) in enumerate(chains):
        o = o_intra[s] + o_state[s]
        o = o * lax.rsqrt(jnp.mean(o * o, axis=-1, keepdims=True) + RMS_EPS) * ng_ref[...]
        r = r_ref[bi, :, h * dv:(h + 1) * dv].astype(F32)
        outs.append((o * (r * _sigmoid(r))).astype(o_ref.dtype))
        new_st.append(st_ref[s] * carry[s] + kv[s])
    st_ref[...] = jnp.stack(new_st)
    o_ref[...] = jnp.stack([jnp.concatenate(outs[bi * nh:(bi + 1) * nh], axis=1) for bi in range(bb)])


def gla_core(proj, a_low, wa_hi, wa_lo, b_a, norm_g, batch, seq, d_model, bb):
    nh, L = GLA_HEADS, CHUNK
    dk = d_model // 2 // nh
    dv = d_model // nh
    nc = seq // L
    qw = nh * dk
    bb = min(bb, batch)
    proj = proj.reshape(batch, seq, 3 * d_model)
    a_low = a_low.reshape(batch, seq, LANES)
    out = pl.pallas_call(
        functools.partial(_gla_kernel, nh=nh, dk=dk, dv=dv, L=L),
        grid=(batch // bb, nc),
        in_specs=[pl.BlockSpec((bb, L, qw), lambda g, c: (g, c, 0)),
                  pl.BlockSpec((bb, L, qw), lambda g, c: (g, c, 1)),
                  pl.BlockSpec((bb, L, d_model), lambda g, c: (g, c, 1)),
                  pl.BlockSpec((bb, L, d_model), lambda g, c: (g, c, 2)),
                  pl.BlockSpec((bb, L, LANES), lambda g, c: (g, c, 0)),
                  pl.BlockSpec((LANES, qw), lambda g, c: (0, 0)),
                  pl.BlockSpec((LANES, qw), lambda g, c: (0, 0)),
                  pl.BlockSpec((1, qw), lambda g, c: (0, 0)),
                  pl.BlockSpec((1, dv), lambda g, c: (0, 0))],
        out_specs=pl.BlockSpec((bb, L, d_model), lambda g, c: (g, c, 0)),
        out_shape=jax.ShapeDtypeStruct((batch, seq, d_model), BF16),
        scratch_shapes=[pltpu.VMEM((bb * nh, dv, dk), F32)],
        compiler_params=_params("parallel", "arbitrary"),
        name="gla_core",
    )(proj, proj, proj, proj, a_low, wa_hi, wa_lo, b_a, norm_g)
    return out.reshape(batch * seq, d_model)


def _mix_xattn_kernel(y_ref, hf_ref, wm_ref, g0_ref, b0_ref, wq_ref, kv_ref, wo_ref, g1_ref, b1_ref,
                      of_ref, ob_ref, *, nh, alpha):
    d = wq_ref.shape[1]
    hd = d // nh
    h1 = _layer_norm(alpha * hf_ref[...] + _dot(y_ref[...], wm_ref[...]), g0_ref[...], b0_ref[...])
    q = (_dot(h1.astype(BF16), wq_ref[...]) * (hd ** -0.5)).astype(BF16)
    outs = []
    for h in range(nh):
        kh = kv_ref[:, h * hd:(h + 1) * hd]
        vh = kv_ref[:, d + h * hd:d + (h + 1) * hd]
        s = _dot_nt(q[:, h * hd:(h + 1) * hd], kh)
        p = jnp.exp(s - jnp.max(s, axis=-1, keepdims=True))
        l = jnp.sum(p, axis=-1, keepdims=True)
        outs.append((_dot(p.astype(BF16), vh) / l).astype(BF16))
    o = jnp.concatenate(outs, axis=-1)
    h2 = _layer_norm(alpha * h1 + _dot(o, wo_ref[...]), g1_ref[...], b1_ref[...])
    of_ref[...] = h2
    ob_ref[...] = h2.astype(BF16)


def mix_xattn_block(y, hf, w_mix, g0, b0, wq, kv, wo, g1, b1, layer, batch, seq, n_mem, alpha, tm):
    d = hf.shape[1]
    tm = min(tm, seq)
    nt = seq // tm
    rows = pl.BlockSpec((tm, d), lambda bi, i: (bi * nt + i, 0))
    whole = lambda shape: pl.BlockSpec(shape, lambda bi, i: (0, 0), pipeline_mode=pl.Buffered(1))
    weight = pl.BlockSpec((None, d, d), lambda bi, i: (layer, 0, 0), pipeline_mode=pl.Buffered(1))
    return pl.pallas_call(
        functools.partial(_mix_xattn_kernel, nh=XA_HEADS, alpha=alpha),
        grid=(batch, nt),
        in_specs=[rows, rows, weight, whole((1, d)), whole((1, d)),
                  weight, pl.BlockSpec((n_mem, 2 * d), lambda bi, i: (bi, 0)), weight,
                  whole((1, d)), whole((1, d))],
        out_specs=[rows, rows],
        out_shape=[jax.ShapeDtypeStruct((batch * seq, d), F32),
                   jax.ShapeDtypeStruct((batch * seq, d), BF16)],
        compiler_params=_params("parallel", "parallel"),
        name="mix_xattn_block",
    )(y, hf, w_mix, g0, b0, wq, kv, wo, g1, b1)


def _gelu_tanh(x):
    return 0.5 * x * (1.0 + jnp.tanh(0.7978845608028654 * (x + 0.044715 * (x * x * x))))


def _ffn_kernel(hb_ref, hf_ref, wu_ref, cw_ref, cb_ref, wd_ref, g_ref, b_ref, of_ref, ob_ref, tail_ref,
                *, tm, tf, group, blocks_per_seq, alpha):
    i = pl.program_id(0)
    f = wd_ref.shape[0]

    @pl.when(i % blocks_per_seq == 0)
    def _():
        tail_ref[...] = jnp.zeros_like(tail_ref)

    hb = hb_ref[...]
    row = lax.broadcasted_iota(jnp.int32, (SUBLANES, tf), 0)

    def up(c):
        return [_dot(hb, wu_ref[:, lo:lo + tf]) for lo in (c * tf, f + c * tf)]

    def conv(u, lo):
        cols = slice(lo, lo + tf)
        tail = tail_ref[:, cols]
        tail_ref[:, cols] = u[tm - SUBLANES:, :]
        p1 = tail[SUBLANES - 1:SUBLANES, :]
        p2 = tail[SUBLANES - 2:SUBLANES - 1, :]
        r1 = pltpu.roll(u, 1, 0)
        r2 = pltpu.roll(u, 2, 0)
        top1 = jnp.where(row == 0, p1, r1[:SUBLANES])
        top2 = jnp.where(row == 0, p2, jnp.where(row == 1, p1, r2[:SUBLANES]))
        u1 = jnp.concatenate([top1, r1[SUBLANES:]], axis=0)
        u2 = jnp.concatenate([top2, r2[SUBLANES:]], axis=0)
        c = cw_ref[:, cols]
        return c[0:1, :] * u2 + c[1:2, :] * u1 + c[2:3, :] * u + cb_ref[:, cols]

    n = f // tf
    y = alpha * hf_ref[...]
    u_next = up(0)
    acts, first = [], 0
    for c in range(n):
        u_gate, u_val = u_next
        if c + 1 < n:
            u_next = up(c + 1)
        acts.append((_gelu_tanh(conv(u_gate, c * tf)) * conv(u_val, f + c * tf)).astype(BF16))
        if len(acts) == group or c + 1 == n:
            y = y + _dot(jnp.concatenate(acts, axis=1), wd_ref[first * tf:(c + 1) * tf, :])
            acts, first = [], c + 1
    y = _layer_norm(y, g_ref[...], b_ref[...])
    of_ref[...] = y
    ob_ref[...] = y.astype(BF16)


def ffn_block(hb, hf, w_up, conv_w, conv_b, w_down, g, b, layer, seq, alpha, tm, tf, group):
    t, d = hb.shape
    f = w_down.shape[1]
    tm = min(tm, seq)
    whole = lambda shape: pl.BlockSpec(shape, lambda i: (0, 0), pipeline_mode=pl.Buffered(1))
    weight = lambda shape: pl.BlockSpec((None,) + shape, lambda i: (layer, 0, 0), pipeline_mode=pl.Buffered(1))
    return pl.pallas_call(
        functools.partial(_ffn_kernel, tm=tm, tf=tf, group=group, blocks_per_seq=seq // tm, alpha=alpha),
        grid=(t // tm,),
        in_specs=[pl.BlockSpec((tm, d), lambda i: (i, 0)),
                  pl.BlockSpec((tm, d), lambda i: (i, 0)),
                  weight((d, 2 * f)),
                  whole((CONV_W, 2 * f)),
                  whole((1, 2 * f)),
                  weight((f, d)),
                  whole((1, d)),
                  whole((1, d))],
        out_specs=[pl.BlockSpec((tm, d), lambda i: (i, 0)),
                   pl.BlockSpec((tm, d), lambda i: (i, 0))],
        out_shape=[jax.ShapeDtypeStruct((t, d), F32),
                   jax.ShapeDtypeStruct((t, d), BF16)],
        scratch_shapes=[pltpu.VMEM((SUBLANES, 2 * f), F32)],
        compiler_params=_params("arbitrary"),
        name="ffn_block",
    )(hb, hf, w_up, conv_w, conv_b, w_down, g, b)


def _pad_cols(w, width):
    return jnp.pad(w, ((0, 0), (0, width - w.shape[1])))


def kernel(x, mem, ln_g, ln_b, mix_wo, sb_win, fox_win, fox_bf, ml_win, ml_bi, ml_bf,
           gla_win, gla_wa2, gla_ba, gla_norm_g, xa_wq, xa_wkv, xa_wo,
           ffn_up, ffn_conv, ffn_conv_b, ffn_down):
    batch, seq, d = x.shape
    n_mem = mem.shape[1]
    depth = mix_wo.shape[0]
    t = batch * seq
    alpha = (2.0 * depth) ** 0.25
    main = 3 * d

    hf = x.reshape(t, d)
    hb = hf
    memb = mem.reshape(batch * n_mem, d).astype(BF16)
    in_w = [w.astype(BF16) for w in (sb_win, fox_win, ml_win, gla_win)]
    mix_wo_b, xa_wq_b, xa_wkv_b, xa_wo_b = (w.astype(BF16) for w in (mix_wo, xa_wq, xa_wkv, xa_wo))
    ffn_up_b, ffn_down_b = ffn_up.astype(BF16), ffn_down.astype(BF16)

    for layer in range(depth):
        kind = layer % 4
        occ = layer // 4
        if kind == 0:
            proj = linear(hb, in_w[0], occ, main, BF16, tm=1024, tn=1024)
            y = sb_attention(proj, batch, seq, d)
        else:
            w_narrow = (fox_win, ml_win, gla_win)[kind - 1][occ][:, main:]
            proj, narrow = in_proj(hb, in_w[kind], occ, main, _pad_cols(w_narrow, LANES).astype(BF16),
                                   tm=1024, tn=1024)
        if kind == 1:
            cum = fox_cum(narrow.reshape(batch, seq, LANES), _pad_cols(fox_bf[occ][None, :], LANES))
            y = fox_attention(proj, cum[:, :, :ATT_HEADS], batch, seq, d)
        elif kind == 2:
            gates = narrow[:, :2 * ML_HEADS] + jnp.concatenate([ml_bi[occ], ml_bf[occ]])[None, :]
            y = mlstm_core(proj, gates.reshape(batch, seq, 2 * ML_HEADS), batch, seq, d, bb=4)
        elif kind == 3:
            wa = jnp.pad(gla_wa2[occ], ((0, LANES - GLA_RANK), (0, 0)))
            wa_hi = wa.astype(BF16)
            wa_lo = (wa - wa_hi.astype(F32)).astype(BF16)
            y = gla_core(proj, narrow, wa_hi, wa_lo, gla_ba[occ][None, :], gla_norm_g[occ][None, :],
                         batch, seq, d, bb=4)
        kv = linear(memb, xa_wkv_b, layer, 2 * d, BF16, tm=1024, tn=1024)
        hf, hb = mix_xattn_block(y, hf, mix_wo_b, ln_g[layer, 0][None, :], ln_b[layer, 0][None, :],
                                 xa_wq_b, kv, xa_wo_b, ln_g[layer, 1][None, :], ln_b[layer, 1][None, :],
                                 layer, batch, seq, n_mem, alpha, tm=512)
        hf, hb = ffn_block(hb, hf, ffn_up_b, ffn_conv[layer, :, 0, :], ffn_conv_b[layer][None, :], ffn_down_b,
                           ln_g[layer, 2][None, :], ln_b[layer, 2][None, :],
                           layer, seq, alpha, tm=512, tf=256, group=4)
    return hf.reshape(batch, seq, d)
```

```python
import functools

import jax
import jax.numpy as jnp
from jax import lax
from jax.experimental import pallas as pl
from jax.experimental.pallas import tpu as pltpu

F32 = jnp.float32
BF16 = jnp.bfloat16

ATT_HEADS = 16
ML_HEADS = 4
GLA_HEADS = 4
GLA_RANK = 16
GLA_TAU = 16.0
XA_HEADS = 4
CHUNK = 64
CONV_W = 3
LN_EPS = 1e-5
RMS_EPS = 1e-6

LANES = 128
SUBLANES = 8
VMEM_LIMIT = 48 * 1024 * 1024

_NT = (((1,), (1,)), ((), ()))
_TN = (((0,), (0,)), ((), ()))


def _params(*sem):
    return pltpu.CompilerParams(dimension_semantics=sem, vmem_limit_bytes=VMEM_LIMIT)


def _dot(a, b):
    return jnp.dot(a, b, preferred_element_type=F32)


def _dot_nt(a, b):
    return lax.dot_general(a, b, _NT, preferred_element_type=F32)


def _dot_tn(a, b):
    return lax.dot_general(a, b, _TN, preferred_element_type=F32)


def _log_sigmoid(x):
    return jnp.minimum(x, 0.0) - jnp.log1p(jnp.exp(-jnp.abs(x)))


def _sigmoid(x):
    return 1.0 / (1.0 + jnp.exp(-x))


def _split2(x):
    hi = x.astype(BF16)
    lo = (x - hi.astype(F32)).astype(BF16)
    return hi, lo


def _split3(x):
    hi = x.astype(BF16)
    r = x - hi.astype(F32)
    mid = r.astype(BF16)
    lo = (r - mid.astype(F32)).astype(BF16)
    return hi, mid, lo


def _layer_norm(z, g, b):
    mu = jnp.mean(z, axis=-1, keepdims=True)
    zc = z - mu
    var = jnp.mean(zc * zc, axis=-1, keepdims=True)
    return zc * lax.rsqrt(var + LN_EPS) * g + b


def _linear_kernel(x_ref, w_ref, o_ref):
    o_ref[...] = _dot(x_ref[...].astype(BF16), w_ref[...]).astype(o_ref.dtype)


def linear(x, w, layer, n, out_dtype, tm, tn):
    m, k = x.shape
    tm = min(tm, m)
    tn = min(tn, n)
    return pl.pallas_call(
        _linear_kernel,
        grid=(m // tm, n // tn),
        in_specs=[pl.BlockSpec((tm, k), lambda i, j: (i, 0)),
                  pl.BlockSpec((None, k, tn), lambda i, j: (layer, 0, j))],
        out_specs=pl.BlockSpec((tm, tn), lambda i, j: (i, j)),
        out_shape=jax.ShapeDtypeStruct((m, n), out_dtype),
        compiler_params=_params("parallel", "arbitrary"),
        name="linear",
    )(x, w)


def _in_proj_kernel(x_ref, w_ref, wg_ref, o_ref, og_ref):
    x = x_ref[...].astype(BF16)
    o_ref[...] = _dot(x, w_ref[...]).astype(o_ref.dtype)

    @pl.when(pl.program_id(1) == 0)
    def _():
        og_ref[...] = _dot(x, wg_ref[...])


def in_proj(x, w, layer, n, wg, tm, tn):
    m, k = x.shape
    tm = min(tm, m)
    tn = min(tn, n)
    return pl.pallas_call(
        _in_proj_kernel,
        grid=(m // tm, n // tn),
        in_specs=[pl.BlockSpec((tm, k), lambda i, j: (i, 0)),
                  pl.BlockSpec((None, k, tn), lambda i, j: (layer, 0, j)),
                  pl.BlockSpec((k, LANES), lambda i, j: (0, 0))],
        out_specs=[pl.BlockSpec((tm, tn), lambda i, j: (i, j)),
                   pl.BlockSpec((tm, LANES), lambda i, j: (i, 0))],
        out_shape=[jax.ShapeDtypeStruct((m, n), BF16),
                   jax.ShapeDtypeStruct((m, LANES), F32)],
        compiler_params=_params("parallel", "arbitrary"),
        name="in_proj",
    )(x, w, wg)


def _head_queries(q_ref, tq, hd):
    lane = lax.broadcasted_iota(jnp.int32, (tq, LANES), 1)
    qs = q_ref[...] * (hd ** -0.5)
    zero = jnp.zeros_like(qs)
    return lane, (jnp.where(lane < hd, qs, zero), jnp.where(lane >= hd, qs, zero))


def _sweep(i, tq, tk, scores, finish, masks, ascending):
    nd = tq // tk
    nfull = i * nd
    per_trip = 2 if nd % 2 == 0 else 1

    def diagonal():
        for c in (range(nd) if ascending else reversed(range(nd))):
            start = pl.multiple_of(i * tq + c * tk, tk)
            finish(scores(start, c * tk), start, c * tk, masks[c])

    def full_body(j, carry):
        blocks = [per_trip * j + u if ascending else nfull - 1 - per_trip * j - u for u in range(per_trip)]
        starts = [pl.multiple_of(kb * tk, tk) for kb in blocks]
        z = [scores(start, 0) for start in starts]
        for start, zz in zip(starts, z):
            finish(zz, start, 0, None)
        return carry

    if ascending:
        lax.fori_loop(0, nfull // per_trip, full_body, 0)
        diagonal()
    else:
        diagonal()
        lax.fori_loop(0, nfull // per_trip, full_body, 0)


def _diag_masks(tq, tk, inclusive):
    masks = []
    for c in range(tq // tk):
        n = tq - c * tk
        col = lax.broadcasted_iota(jnp.int32, (n, tk), 1)
        row = lax.broadcasted_iota(jnp.int32, (n, tk), 0)
        masks.append(col <= row if inclusive else col < row)
    return masks


def _sb_kernel(q_ref, k_ref, v_ref, o_ref, acc_ref, run_ref, *, tq, tk, hd):
    i = pl.program_id(2)
    lane, q_heads = _head_queries(q_ref, tq, hd)
    r = lax.broadcasted_iota(jnp.int32, (tk, tk), 0)
    c = lax.broadcasted_iota(jnp.int32, (tk, tk), 1)
    neg_later = jnp.where(r > c, -1.0, 0.0).astype(BF16)
    acc_ref[...] = jnp.zeros_like(acc_ref)
    run_ref[...] = jnp.zeros_like(run_ref)

    def scores(start, lo):
        ks = k_ref[pl.ds(start, tk), :]
        return [_dot_nt(q_heads[hh][lo:], ks) for hh in range(2)]

    def finish(z, start, lo, mask):
        vs = v_ref[pl.ds(start, tk), :]
        for hh in range(2):
            zb = z[hh].astype(BF16)
            soft = jnp.log(1.0 + jnp.exp(-jnp.abs(zb)))
            drop = jnp.maximum(zb, 0.0) + soft
            if mask is not None:
                drop = jnp.where(mask, drop, jnp.zeros_like(drop))
            suffix = _dot(drop, neg_later)
            total = suffix[:, 0:1] - drop[:, 0:1].astype(F32)
            run = run_ref[hh, lo:, :]
            log_sig = jnp.minimum(z[hh], 0.0) - soft.astype(F32)
            w = jnp.exp(log_sig + suffix + jnp.concatenate([run] * (tk // LANES), axis=1))
            if mask is not None:
                w = jnp.where(mask, w, 0.0)
            acc_ref[hh, lo:, :] += _dot(w.astype(BF16), vs)
            run_ref[hh, lo:, :] = run + total

    _sweep(i, tq, tk, scores, finish, _diag_masks(tq, tk, inclusive=False), ascending=False)
    o_ref[...] = jnp.where(lane < hd, acc_ref[0], acc_ref[1]).astype(o_ref.dtype)


def sb_attention(qkv, batch, seq, d_model, tq=2048, tk=256):
    hd = d_model // ATT_HEADS
    npair = d_model // LANES
    tq = min(tq, seq)
    nq = seq // tq
    return pl.pallas_call(
        functools.partial(_sb_kernel, tq=tq, tk=tk, hd=hd),
        grid=(batch, npair, nq),
        in_specs=[pl.BlockSpec((tq, LANES), lambda b, p, i: (b * nq + i, p)),
                  pl.BlockSpec((seq, LANES), lambda b, p, i: (b, npair + p)),
                  pl.BlockSpec((seq, LANES), lambda b, p, i: (b, 2 * npair + p))],
        out_specs=pl.BlockSpec((tq, LANES), lambda b, p, i: (b * nq + i, p)),
        out_shape=jax.ShapeDtypeStruct((batch * seq, d_model), BF16),
        scratch_shapes=[pltpu.VMEM((2, tq, LANES), F32), pltpu.VMEM((2, tq, LANES), F32)],
        compiler_params=_params("parallel", "parallel", "arbitrary"),
        name="sb_attention",
    )(qkv, qkv, qkv)


def _fox_cum_kernel(f_ref, b_ref, o_ref, *, seq, blk):
    row = lax.broadcasted_iota(jnp.int32, (blk, blk), 0)
    col = lax.broadcasted_iota(jnp.int32, (blk, blk), 1)
    tri = jnp.where(col <= row, 1.0, 0.0).astype(BF16)
    run = jnp.zeros((1, f_ref.shape[-1]), F32)
    for c in range(seq // blk):
        lf = _log_sigmoid(f_ref[0, c * blk:(c + 1) * blk, :] + b_ref[...])
        hi, mid, lo = _split3(lf)
        cs = _dot(tri, hi) + _dot(tri, mid) + _dot(tri, lo) + run
        o_ref[0, c * blk:(c + 1) * blk, :] = cs
        run = cs[blk - 1:blk, :]


def fox_cum(f_pre, b_f, blk=256):
    batch, seq, nh = f_pre.shape
    blk = min(blk, seq)
    return pl.pallas_call(
        functools.partial(_fox_cum_kernel, seq=seq, blk=blk),
        grid=(batch,),
        in_specs=[pl.BlockSpec((1, seq, nh), lambda b: (b, 0, 0)),
                  pl.BlockSpec((1, nh), lambda b: (0, 0))],
        out_specs=pl.BlockSpec((1, seq, nh), lambda b: (b, 0, 0)),
        out_shape=jax.ShapeDtypeStruct((batch, seq, nh), F32),
        compiler_params=_params("parallel"),
        name="fox_cum",
    )(f_pre, b_f)


def _fox_kernel(q_ref, k_ref, v_ref, cr_ref, o_ref, acc_ref, m_ref, *, tq, tk, hd):
    i = pl.program_id(2)
    lane, q_heads = _head_queries(q_ref, tq, hd)
    klane = lax.broadcasted_iota(jnp.int32, (tk, LANES), 1)
    in_head = (klane < hd, klane >= hd)
    acc_ref[...] = jnp.zeros_like(acc_ref)
    m_ref[...] = jnp.full_like(m_ref, -jnp.inf)

    def scores(start, lo):
        ks = k_ref[pl.ds(start, tk), :]
        return [_dot_nt(q_heads[hh][lo:], ks) for hh in range(2)]

    def finish(s, start, lo, mask):
        vs = v_ref[pl.ds(start, tk), :]
        for hh in range(2):
            vh = jnp.where(in_head[hh], vs, jnp.ones_like(vs))
            cum_row = cr_ref[0, 0, hh:hh + 1, pl.ds(start, tk)]
            z = s[hh] - cum_row
            if mask is not None:
                z = jnp.where(mask, z, -jnp.inf)
            m_prev = m_ref[hh, lo:, :]
            m_new = jnp.maximum(m_prev, jnp.max(z, axis=-1, keepdims=True))
            a = jnp.exp(m_prev - m_new)
            p = jnp.exp(z - jnp.concatenate([m_new] * (tk // LANES), axis=1)).astype(BF16)
            m_ref[hh, lo:, :] = m_new
            acc_ref[hh, lo:, :] = a * acc_ref[hh, lo:, :] + _dot(p, vh)

    _sweep(i, tq, tk, scores, finish, _diag_masks(tq, tk, inclusive=True), ascending=True)
    acc0 = acc_ref[0]
    acc1 = acc_ref[1]
    o = jnp.where(lane < hd, acc0 / acc0[:, hd:hd + 1], acc1 / acc1[:, 0:1])
    o_ref[...] = o.astype(o_ref.dtype)


def fox_attention(qkv, cum, batch, seq, d_model, tq=2048, tk=256):
    hd = d_model // ATT_HEADS
    npair = d_model // LANES
    tq = min(tq, seq)
    nq = seq // tq
    cum_row = cum.reshape(batch, seq, npair, 2).transpose(0, 2, 3, 1)
    return pl.pallas_call(
        functools.partial(_fox_kernel, tq=tq, tk=tk, hd=hd),
        grid=(batch, npair, nq),
        in_specs=[pl.BlockSpec((tq, LANES), lambda b, p, i: (b * nq + i, p)),
                  pl.BlockSpec((seq, LANES), lambda b, p, i: (b, npair + p)),
                  pl.BlockSpec((seq, LANES), lambda b, p, i: (b, 2 * npair + p)),
                  pl.BlockSpec((1, 1, 2, seq), lambda b, p, i: (b, p, 0, 0))],
        out_specs=pl.BlockSpec((tq, LANES), lambda b, p, i: (b * nq + i, p)),
        out_shape=jax.ShapeDtypeStruct((batch * seq, d_model), BF16),
        scratch_shapes=[pltpu.VMEM((2, tq, LANES), F32), pltpu.VMEM((2, tq, LANES), F32)],
        compiler_params=_params("parallel", "parallel", "arbitrary"),
        name="fox_attention",
    )(qkv, qkv, qkv, cum_row)


def _lanes(x, width):
    return x if width == LANES else jnp.concatenate([x] * (width // LANES), axis=1)


def _mlstm_kernel(q_ref, k_ref, v_ref, og_ref, gc_ref, gr_ref, o_ref, c_ref, m_ref,
                  *, nh, dk, dv, L):
    @pl.when(pl.program_id(1) == 0)
    def _():
        c_ref[...] = jnp.zeros_like(c_ref)
        m_ref[...] = jnp.zeros_like(m_ref)

    row = lax.broadcasted_iota(jnp.int32, (L, L), 0)
    col = lax.broadcasted_iota(jnp.int32, (L, L), 1)
    causal = col <= row
    scale = dk ** -0.5
    ones = jnp.ones((L, LANES), BF16)
    bb = q_ref.shape[0]
    chains = [(bi, h) for bi in range(bb) for h in range(nh)]
    n_chains = len(chains)
    q = [q_ref[bi, :, h * dk:(h + 1) * dk] for bi, h in chains]
    k = [k_ref[bi, :, h * dk:(h + 1) * dk] for bi, h in chains]
    v = [jnp.concatenate([v_ref[bi, :, h * dv:(h + 1) * dv], ones], axis=1) for bi, h in chains]

    qk_raw = [_dot_nt(q[s], k[s]) for s in range(n_chains)]
    q_state = [_dot(q[s], c_ref[s].astype(BF16)) for s in range(n_chains)]

    w_intra, w_inter, floor, m_new, decay, kw = [], [], [], [], [], []
    for s, (bi, h) in enumerate(chains):
        gcol = gc_ref[bi]
        grow = gr_ref[bi, 0]
        i_col = gcol[:, h:h + 1]
        i_row = grow[h:h + 1, :]
        lf_col = _log_sigmoid(gcol[:, nh + h:nh + h + 1])
        lf_row = _log_sigmoid(grow[nh + h:nh + h + 1, :])
        hi, lo = _split2(jnp.where(causal, lf_row, 0.0))
        bcum = _dot(hi, ones) + _dot(lo, ones)
        bcum_row = jnp.sum(jnp.where(row <= col, lf_col, 0.0), axis=0, keepdims=True)
        m_prev = m_ref[s]
        d = jnp.where(causal, bcum[:, :L] - bcum_row + i_row, -jnp.inf)
        inter = bcum + m_prev
        mt = jnp.maximum(inter, jnp.max(d, axis=1, keepdims=True))
        mn = mt[L - 1:L, 0:1]
        blast = bcum[L - 1:L, 0:1]
        m_new.append(mn)
        w_intra.append(jnp.exp(d - mt[:, :L]))
        w_inter.append(jnp.exp(inter - mt))
        floor.append(jnp.exp(-mt))
        decay.append(jnp.exp(blast + m_prev - mn))
        kw.append((k[s].astype(F32) * scale) * jnp.exp(blast - _lanes(bcum, dk) + i_col - mn))

    qk = [(qk_raw[s] * (scale * w_intra[s])).astype(BF16) for s in range(n_chains)]
    intra = [_dot(qk[s], v[s]) for s in range(n_chains)]
    kv = [_dot_tn(kw[s].astype(BF16), v[s]) for s in range(n_chains)]

    new_c, outs = [], []
    for s, (bi, h) in enumerate(chains):
        numden = _lanes(w_inter[s], dv + LANES) * q_state[s] + intra[s]
        den = jnp.maximum(jnp.abs(numden[:, dv:]), floor[s])
        h_out = numden[:, :dv] / _lanes(den, dv)
        new_c.append(decay[s] * c_ref[s] + kv[s])
        gate = _sigmoid(og_ref[bi, :, h * dv:(h + 1) * dv].astype(F32))
        outs.append((h_out * gate).astype(o_ref.dtype))
    c_ref[...] = jnp.stack(new_c)
    m_ref[...] = jnp.stack(m_new)
    o_ref[...] = jnp.stack([jnp.concatenate(outs[bi * nh:(bi + 1) * nh], axis=1) for bi in range(bb)])


def mlstm_core(proj, gates, batch, seq, d_model, bb):
    nh, L = ML_HEADS, CHUNK
    dk = d_model // 2 // nh
    dv = d_model // nh
    nc = seq // L
    qw = nh * dk
    bb = min(bb, batch)
    proj = proj.reshape(batch, seq, 3 * d_model)
    gates_row = gates.reshape(batch, nc, L, 2 * nh).transpose(0, 1, 3, 2)
    out = pl.pallas_call(
        functools.partial(_mlstm_kernel, nh=nh, dk=dk, dv=dv, L=L),
        grid=(batch // bb, nc),
        in_specs=[pl.BlockSpec((bb, L, qw), lambda g, c: (g, c, 0)),
                  pl.BlockSpec((bb, L, qw), lambda g, c: (g, c, 1)),
                  pl.BlockSpec((bb, L, d_model), lambda g, c: (g, c, 1)),
                  pl.BlockSpec((bb, L, d_model), lambda g, c: (g, c, 2)),
                  pl.BlockSpec((bb, L, 2 * nh), lambda g, c: (g, c, 0)),
                  pl.BlockSpec((bb, 1, 2 * nh, L), lambda g, c: (g, c, 0, 0))],
        out_specs=pl.BlockSpec((bb, L, d_model), lambda g, c: (g, c, 0)),
        out_shape=jax.ShapeDtypeStruct((batch, seq, d_model), BF16),
        scratch_shapes=[pltpu.VMEM((bb * nh, dk, dv + LANES), F32),
                        pltpu.VMEM((bb * nh, 1, 1), F32)],
        compiler_params=_params("parallel", "arbitrary"),
        name="mlstm_core",
    )(proj, proj, proj, proj, gates, gates_row)
    return out.reshape(batch * seq, d_model)


def _gla_kernel(q_ref, k_ref, v_ref, r_ref, a_ref, wh_ref, wl_ref, ba_ref, ng_ref, o_ref, st_ref,
                *, nh, dk, dv, L):
    @pl.when(pl.program_id(1) == 0)
    def _():
        st_ref[...] = jnp.zeros_like(st_ref)

    row = lax.broadcasted_iota(jnp.int32, (L, L), 0)
    col = lax.broadcasted_iota(jnp.int32, (L, L), 1)
    causal = col <= row
    tri = jnp.where(causal, 1.0, 0.0).astype(BF16)
    scale = dk ** -0.5
    bb = q_ref.shape[0]
    chains = [(bi, h) for bi in range(bb) for h in range(nh)]

    logits = []
    for bi in range(bb):
        a_hi, a_lo = _split2(a_ref[bi])
        wh = wh_ref[...]
        logits.append(_dot(a_hi, wh) + _dot(a_lo, wh) + _dot(a_hi, wl_ref[...]) + ba_ref[...])

    bcums = []
    for bi in range(bb):
        la_hi, la_lo = _split2(_log_sigmoid(logits[bi]) * (1.0 / GLA_TAU))
        bcums.append(_dot(tri, la_hi) + _dot(tri, la_lo))

    v, q_rel, k_rel, q_dec, k_dec, carry = [], [], [], [], [], []
    for bi, h in chains:
        hs = slice(h * dk, (h + 1) * dk)
        bcum = bcums[bi][:, hs]
        blast = bcum[L - 1:L, :]
        ref = bcum[L // 2 - 1:L // 2, :]
        qf = q_ref[bi, :, hs].astype(F32) * scale
        kf = k_ref[bi, :, hs].astype(F32)
        v.append(v_ref[bi, :, h * dv:(h + 1) * dv])
        q_rel.append((qf * jnp.exp(bcum - ref)).astype(BF16))
        k_rel.append((kf * jnp.exp(ref - bcum)).astype(BF16))
        q_dec.append((qf * jnp.exp(bcum)).astype(BF16))
        k_dec.append((kf * jnp.exp(blast - bcum)).astype(BF16))
        carry.append(jnp.exp(blast))

    n = len(chains)
    att = [jnp.where(causal, _dot_nt(q_rel[s], k_rel[s]), 0.0).astype(BF16) for s in range(n)]
    o_state = [_dot_nt(q_dec[s], st_ref[s].astype(BF16)) for s in range(n)]
    kv = [_dot_tn(v[s], k_dec[s]) for s in range(n)]

    o_intra = [_dot(att[s], v[s]) for s in range(n)]
    new_st, outs = [], []
    for s, (bi, h) in enumerate(chains):
        o = o_intra[s] + o_state[s]
        o = o * lax.rsqrt(jnp.mean(o * o, axis=-1, keepdims=True) + RMS_EPS) * ng_ref[...]
        r = r_ref[bi, :, h * dv:(h + 1) * dv].astype(F32)
        outs.append((o * (r * _sigmoid(r))).astype(o_ref.dtype))
        new_st.append(st_ref[s] * carry[s] + kv[s])
    st_ref[...] = jnp.stack(new_st)
    o_ref[...] = jnp.stack([jnp.concatenate(outs[bi * nh:(bi + 1) * nh], axis=1) for bi in range(bb)])


def gla_core(proj, a_low, wa_hi, wa_lo, b_a, norm_g, batch, seq, d_model, bb):
    nh, L = GLA_HEADS, CHUNK
    dk = d_model // 2 // nh
    dv = d_model // nh
    nc = seq // L
    qw = nh * dk
    bb = min(bb, batch)
    proj = proj.reshape(batch, seq, 3 * d_model)
    a_low = a_low.reshape(batch, seq, LANES)
    out = pl.pallas_call(
        functools.partial(_gla_kernel, nh=nh, dk=dk, dv=dv, L=L),
        grid=(batch // bb, nc),
        in_specs=[pl.BlockSpec((bb, L, qw), lambda g, c: (g, c, 0)),
                  pl.BlockSpec((bb, L, qw), lambda g, c: (g, c, 1)),
                  pl.BlockSpec((bb, L, d_model), lambda g, c: (g, c, 1)),
                  pl.BlockSpec((bb, L, d_model), lambda g, c: (g, c, 2)),
                  pl.BlockSpec((bb, L, LANES), lambda g, c: (g, c, 0)),
                  pl.BlockSpec((LANES, qw), lambda g, c: (0, 0)),
                  pl.BlockSpec((LANES, qw), lambda g, c: (0, 0)),
                  pl.BlockSpec((1, qw), lambda g, c: (0, 0)),
                  pl.BlockSpec((1, dv), lambda g, c: (0, 0))],
        out_specs=pl.BlockSpec((bb, L, d_model), lambda g, c: (g, c, 0)),
        out_shape=jax.ShapeDtypeStruct((batch, seq, d_model), BF16),
        scratch_shapes=[pltpu.VMEM((bb * nh, dv, dk), F32)],
        compiler_params=_params("parallel", "arbitrary"),
        name="gla_core",
    )(proj, proj, proj, proj, a_low, wa_hi, wa_lo, b_a, norm_g)
    return out.reshape(batch * seq, d_model)


def _mix_xattn_kernel(y_ref, hf_ref, wm_ref, g0_ref, b0_ref, wq_ref, kv_ref, wo_ref, g1_ref, b1_ref,
                      of_ref, ob_ref, *, nh, alpha):
    d = wq_ref.shape[1]
    hd = d // nh
    h1 = _layer_norm(alpha * hf_ref[...] + _dot(y_ref[...], wm_ref[...]), g0_ref[...], b0_ref[...])
    q = (_dot(h1.astype(BF16), wq_ref[...]) * (hd ** -0.5)).astype(BF16)
    outs = []
    for h in range(nh):
        kh = kv_ref[:, h * hd:(h + 1) * hd]
        vh = kv_ref[:, d + h * hd:d + (h + 1) * hd]
        s = _dot_nt(q[:, h * hd:(h + 1) * hd], kh)
        p = jnp.exp(s - jnp.max(s, axis=-1, keepdims=True))
        l = jnp.sum(p, axis=-1, keepdims=True)
        outs.append((_dot(p.astype(BF16), vh) / l).astype(BF16))
    o = jnp.concatenate(outs, axis=-1)
    h2 = _layer_norm(alpha * h1 + _dot(o, wo_ref[...]), g1_ref[...], b1_ref[...])
    of_ref[...] = h2
    ob_ref[...] = h2.astype(BF16)


def mix_xattn_block(y, hf, w_mix, g0, b0, wq, kv, wo, g1, b1, layer, batch, seq, n_mem, alpha, tm):
    d = hf.shape[1]
    tm = min(tm, seq)
    nt = seq // tm
    rows = pl.BlockSpec((tm, d), lambda bi, i: (bi * nt + i, 0))
    whole = lambda shape: pl.BlockSpec(shape, lambda bi, i: (0, 0), pipeline_mode=pl.Buffered(1))
    weight = pl.BlockSpec((None, d, d), lambda bi, i: (layer, 0, 0), pipeline_mode=pl.Buffered(1))
    return pl.pallas_call(
        functools.partial(_mix_xattn_kernel, nh=XA_HEADS, alpha=alpha),
        grid=(batch, nt),
        in_specs=[rows, rows, weight, whole((1, d)), whole((1, d)),
                  weight, pl.BlockSpec((n_mem, 2 * d), lambda bi, i: (bi, 0)), weight,
                  whole((1, d)), whole((1, d))],
        out_specs=[rows, rows],
        out_shape=[jax.ShapeDtypeStruct((batch * seq, d), F32),
                   jax.ShapeDtypeStruct((batch * seq, d), BF16)],
        compiler_params=_params("parallel", "parallel"),
        name="mix_xattn_block",
    )(y, hf, w_mix, g0, b0, wq, kv, wo, g1, b1)


def _gelu_tanh(x):
    return 0.5 * x * (1.0 + jnp.tanh(0.7978845608028654 * (x + 0.044715 * (x * x * x))))


def _ffn_kernel(hb_ref, hf_ref, wu_ref, cw_ref, cb_ref, wd_ref, g_ref, b_ref, of_ref, ob_ref, tail_ref,
                *, tm, tf, group, blocks_per_seq, alpha):
    i = pl.program_id(0)
    f = wd_ref.shape[0]

    @pl.when(i % blocks_per_seq == 0)
    def _():
        tail_ref[...] = jnp.zeros_like(tail_ref)

    hb = hb_ref[...]
    row = lax.broadcasted_iota(jnp.int32, (SUBLANES, tf), 0)

    def up(c):
        return [_dot(hb, wu_ref[:, lo:lo + tf]) for lo in (c * tf, f + c * tf)]

    def conv(u, lo):
        cols = slice(lo, lo + tf)
        tail = tail_ref[:, cols]
        tail_ref[:, cols] = u[tm - SUBLANES:, :]
        p1 = tail[SUBLANES - 1:SUBLANES, :]
        p2 = tail[SUBLANES - 2:SUBLANES - 1, :]
        r1 = pltpu.roll(u, 1, 0)
        r2 = pltpu.roll(u, 2, 0)
        top1 = jnp.where(row == 0, p1, r1[:SUBLANES])
        top2 = jnp.where(row == 0, p2, jnp.where(row == 1, p1, r2[:SUBLANES]))
        u1 = jnp.concatenate([top1, r1[SUBLANES:]], axis=0)
        u2 = jnp.concatenate([top2, r2[SUBLANES:]], axis=0)
        c = cw_ref[:, cols]
        return c[0:1, :] * u2 + c[1:2, :] * u1 + c[2:3, :] * u + cb_ref[:, cols]

    n = f // tf
    y = alpha * hf_ref[...]
    u_next = up(0)
    acts, first = [], 0
    for c in range(n):
        u_gate, u_val = u_next
        if c + 1 < n:
            u_next = up(c + 1)
        acts.append((_gelu_tanh(conv(u_gate, c * tf)) * conv(u_val, f + c * tf)).astype(BF16))
        if len(acts) == group or c + 1 == n:
            y = y + _dot(jnp.concatenate(acts, axis=1), wd_ref[first * tf:(c + 1) * tf, :])
            acts, first = [], c + 1
    y = _layer_norm(y, g_ref[...], b_ref[...])
    of_ref[...] = y
    ob_ref[...] = y.astype(BF16)


def ffn_block(hb, hf, w_up, conv_w, conv_b, w_down, g, b, layer, seq, alpha, tm, tf, group):
    t, d = hb.shape
    f = w_down.shape[1]
    tm = min(tm, seq)
    whole = lambda shape: pl.BlockSpec(shape, lambda i: (0, 0), pipeline_mode=pl.Buffered(1))
    weight = lambda shape: pl.BlockSpec((None,) + shape, lambda i: (layer, 0, 0), pipeline_mode=pl.Buffered(1))
    return pl.pallas_call(
        functools.partial(_ffn_kernel, tm=tm, tf=tf, group=group, blocks_per_seq=seq // tm, alpha=alpha),
        grid=(t // tm,),
        in_specs=[pl.BlockSpec((tm, d), lambda i: (i, 0)),
                  pl.BlockSpec((tm, d), lambda i: (i, 0)),
                  weight((d, 2 * f)),
                  whole((CONV_W, 2 * f)),
                  whole((1, 2 * f)),
                  weight((f, d)),
                  whole((1, d)),
                  whole((1, d))],
        out_specs=[pl.BlockSpec((tm, d), lambda i: (i, 0)),
                   pl.BlockSpec((tm, d), lambda i: (i, 0))],
        out_shape=[jax.ShapeDtypeStruct((t, d), F32),
                   jax.ShapeDtypeStruct((t, d), BF16)],
        scratch_shapes=[pltpu.VMEM((SUBLANES, 2 * f), F32)],
        compiler_params=_params("arbitrary"),
        name="ffn_block",
    )(hb, hf, w_up, conv_w, conv_b, w_down, g, b)


def _pad_cols(w, width):
    return jnp.pad(w, ((0, 0), (0, width - w.shape[1])))


def kernel(x, mem, ln_g, ln_b, mix_wo, sb_win, fox_win, fox_bf, ml_win, ml_bi, ml_bf,
           gla_win, gla_wa2, gla_ba, gla_norm_g, xa_wq, xa_wkv, xa_wo,
           ffn_up, ffn_conv, ffn_conv_b, ffn_down):
    batch, seq, d = x.shape
    n_mem = mem.shape[1]
    depth = mix_wo.shape[0]
    t = batch * seq
    alpha = (2.0 * depth) ** 0.25
    main = 3 * d

    hf = x.reshape(t, d)
    hb = hf
    memb = mem.reshape(batch * n_mem, d).astype(BF16)
    in_w = [w.astype(BF16) for w in (sb_win, fox_win, ml_win, gla_win)]
    mix_wo_b, xa_wq_b, xa_wkv_b, xa_wo_b = (w.astype(BF16) for w in (mix_wo, xa_wq, xa_wkv, xa_wo))
    ffn_up_b, ffn_down_b = ffn_up.astype(BF16), ffn_down.astype(BF16)

    for layer in range(depth):
        kind = layer % 4
        occ = layer // 4
        if kind == 0:
            proj = linear(hb, in_w[0], occ, main, BF16, tm=1024, tn=1024)
            y = sb_attention(proj, batch, seq, d)
        else:
            w_narrow = (fox_win, ml_win, gla_win)[kind - 1][occ][:, main:]
            proj, narrow = in_proj(hb, in_w[kind], occ, main, _pad_cols(w_narrow, LANES).astype(BF16),
                                   tm=1024, tn=1024)
        if kind == 1:
            cum = fox_cum(narrow.reshape(batch, seq, LANES), _pad_cols(fox_bf[occ][None, :], LANES))
            y = fox_attention(proj, cum[:, :, :ATT_HEADS], batch, seq, d)
        elif kind == 2:
            gates = narrow[:, :2 * ML_HEADS] + jnp.concatenate([ml_bi[occ], ml_bf[occ]])[None, :]
            y = mlstm_core(proj, gates.reshape(batch, seq, 2 * ML_HEADS), batch, seq, d, bb=8)
        elif kind == 3:
            wa = jnp.pad(gla_wa2[occ], ((0, LANES - GLA_RANK), (0, 0)))
            wa_hi = wa.astype(BF16)
            wa_lo = (wa - wa_hi.astype(F32)).astype(BF16)
            y = gla_core(proj, narrow, wa_hi, wa_lo, gla_ba[occ][None, :], gla_norm_g[occ][None, :],
                         batch, seq, d, bb=8)
        kv = linear(memb, xa_wkv_b, layer, 2 * d, BF16, tm=1024, tn=1024)
        hf, hb = mix_xattn_block(y, hf, mix_wo_b, ln_g[layer, 0][None, :], ln_b[layer, 0][None, :],
                                 xa_wq_b, kv, xa_wo_b, ln_g[layer, 1][None, :], ln_b[layer, 1][None, :],
                                 layer, batch, seq, n_mem, alpha, tm=512)
        hf, hb = ffn_block(hb, hf, ffn_up_b, ffn_conv[layer, :, 0, :], ffn_conv_b[layer][None, :], ffn_down_b,
                           ln_g[layer, 2][None, :], ln_b[layer, 2][None, :],
                           layer, seq, alpha, tm=512, tf=256, group=4)
    return hf.reshape(batch, seq, d)
```

```python
import functools

import jax
import jax.numpy as jnp
from jax import lax
from jax.experimental import pallas as pl
from jax.experimental.pallas import tpu as pltpu

F32 = jnp.float32
BF16 = jnp.bfloat16

ATT_HEADS = 16
ML_HEADS = 4
GLA_HEADS = 4
GLA_RANK = 16
GLA_TAU = 16.0
XA_HEADS = 4
CHUNK = 64
CONV_W = 3
LN_EPS = 1e-5
RMS_EPS = 1e-6

LANES = 128
SUBLANES = 8
VMEM_LIMIT = 48 * 1024 * 1024

_NT = (((1,), (1,)), ((), ()))
_TN = (((0,), (0,)), ((), ()))


def _params(*sem):
    return pltpu.CompilerParams(dimension_semantics=sem, vmem_limit_bytes=VMEM_LIMIT)


def _dot(a, b):
    return jnp.dot(a, b, preferred_element_type=F32)


def _dot_nt(a, b):
    return lax.dot_general(a, b, _NT, preferred_element_type=F32)


def _dot_tn(a, b):
    return lax.dot_general(a, b, _TN, preferred_element_type=F32)


def _log_sigmoid(x):
    return jnp.minimum(x, 0.0) - jnp.log1p(jnp.exp(-jnp.abs(x)))


def _sigmoid(x):
    return 1.0 / (1.0 + jnp.exp(-x))


def _split2(x):
    hi = x.astype(BF16)
    lo = (x - hi.astype(F32)).astype(BF16)
    return hi, lo


def _split3(x):
    hi = x.astype(BF16)
    r = x - hi.astype(F32)
    mid = r.astype(BF16)
    lo = (r - mid.astype(F32)).astype(BF16)
    return hi, mid, lo


def _layer_norm(z, g, b):
    mu = jnp.mean(z, axis=-1, keepdims=True)
    zc = z - mu
    var = jnp.mean(zc * zc, axis=-1, keepdims=True)
    return zc * lax.rsqrt(var + LN_EPS) * g + b


def _linear_kernel(x_ref, w_ref, o_ref):
    o_ref[...] = _dot(x_ref[...].astype(BF16), w_ref[...]).astype(o_ref.dtype)


def linear(x, w, layer, n, out_dtype, tm, tn):
    m, k = x.shape
    tm = min(tm, m)
    tn = min(tn, n)
    return pl.pallas_call(
        _linear_kernel,
        grid=(m // tm, n // tn),
        in_specs=[pl.BlockSpec((tm, k), lambda i, j: (i, 0)),
                  pl.BlockSpec((None, k, tn), lambda i, j: (layer, 0, j))],
        out_specs=pl.BlockSpec((tm, tn), lambda i, j: (i, j)),
        out_shape=jax.ShapeDtypeStruct((m, n), out_dtype),
        compiler_params=_params("parallel", "arbitrary"),
        name="linear",
    )(x, w)


def _in_proj_kernel(x_ref, w_ref, wg_ref, o_ref, og_ref):
    x = x_ref[...].astype(BF16)
    o_ref[...] = _dot(x, w_ref[...]).astype(o_ref.dtype)

    @pl.when(pl.program_id(1) == 0)
    def _():
        og_ref[...] = _dot(x, wg_ref[...])


def in_proj(x, w, layer, n, wg, tm, tn):
    m, k = x.shape
    tm = min(tm, m)
    tn = min(tn, n)
    return pl.pallas_call(
        _in_proj_kernel,
        grid=(m // tm, n // tn),
        in_specs=[pl.BlockSpec((tm, k), lambda i, j: (i, 0)),
                  pl.BlockSpec((None, k, tn), lambda i, j: (layer, 0, j)),
                  pl.BlockSpec((k, LANES), lambda i, j: (0, 0))],
        out_specs=[pl.BlockSpec((tm, tn), lambda i, j: (i, j)),
                   pl.BlockSpec((tm, LANES), lambda i, j: (i, 0))],
        out_shape=[jax.ShapeDtypeStruct((m, n), BF16),
                   jax.ShapeDtypeStruct((m, LANES), F32)],
        compiler_params=_params("parallel", "arbitrary"),
        name="in_proj",
    )(x, w, wg)


def _head_queries(q_ref, tq, hd):
    lane = lax.broadcasted_iota(jnp.int32, (tq, LANES), 1)
    qs = q_ref[...] * (hd ** -0.5)
    zero = jnp.zeros_like(qs)
    return lane, (jnp.where(lane < hd, qs, zero), jnp.where(lane >= hd, qs, zero))


def _sweep(i, tq, tk, scores, finish, masks, ascending):
    nd = tq // tk
    nfull = i * nd
    per_trip = 2 if nd % 2 == 0 else 1

    def diagonal():
        for c in (range(nd) if ascending else reversed(range(nd))):
            start = pl.multiple_of(i * tq + c * tk, tk)
            finish(scores(start, c * tk), start, c * tk, masks[c])

    def full_body(j, carry):
        blocks = [per_trip * j + u if ascending else nfull - 1 - per_trip * j - u for u in range(per_trip)]
        starts = [pl.multiple_of(kb * tk, tk) for kb in blocks]
        z = [scores(start, 0) for start in starts]
        for start, zz in zip(starts, z):
            finish(zz, start, 0, None)
        return carry

    if ascending:
        lax.fori_loop(0, nfull // per_trip, full_body, 0)
        diagonal()
    else:
        diagonal()
        lax.fori_loop(0, nfull // per_trip, full_body, 0)


def _diag_masks(tq, tk, inclusive):
    masks = []
    for c in range(tq // tk):
        n = tq - c * tk
        col = lax.broadcasted_iota(jnp.int32, (n, tk), 1)
        row = lax.broadcasted_iota(jnp.int32, (n, tk), 0)
        masks.append(col <= row if inclusive else col < row)
    return masks


def _sb_kernel(q_ref, k_ref, v_ref, o_ref, acc_ref, run_ref, *, tq, tk, hd):
    i = pl.program_id(2)
    lane, q_heads = _head_queries(q_ref, tq, hd)
    r = lax.broadcasted_iota(jnp.int32, (tk, tk), 0)
    c = lax.broadcasted_iota(jnp.int32, (tk, tk), 1)
    neg_later = jnp.where(r > c, -1.0, 0.0).astype(BF16)
    acc_ref[...] = jnp.zeros_like(acc_ref)
    run_ref[...] = jnp.zeros_like(run_ref)

    def scores(start, lo):
        ks = k_ref[pl.ds(start, tk), :]
        return [_dot_nt(q_heads[hh][lo:], ks) for hh in range(2)]

    def finish(z, start, lo, mask):
        vs = v_ref[pl.ds(start, tk), :]
        for hh in range(2):
            zb = z[hh].astype(BF16)
            soft = jnp.log(1.0 + jnp.exp(-jnp.abs(zb)))
            drop = jnp.maximum(zb, 0.0) + soft
            if mask is not None:
                drop = jnp.where(mask, drop, jnp.zeros_like(drop))
            suffix = _dot(drop, neg_later)
            total = suffix[:, 0:1] - drop[:, 0:1].astype(F32)
            run = run_ref[hh, lo:, :]
            log_sig = jnp.minimum(z[hh], 0.0) - soft.astype(F32)
            w = jnp.exp(log_sig + suffix + jnp.concatenate([run] * (tk // LANES), axis=1))
            if mask is not None:
                w = jnp.where(mask, w, 0.0)
            acc_ref[hh, lo:, :] += _dot(w.astype(BF16), vs)
            run_ref[hh, lo:, :] = run + total

    _sweep(i, tq, tk, scores, finish, _diag_masks(tq, tk, inclusive=False), ascending=False)
    o_ref[...] = jnp.where(lane < hd, acc_ref[0], acc_ref[1]).astype(o_ref.dtype)


def sb_attention(qkv, batch, seq, d_model, tq=2048, tk=256):
    hd = d_model // ATT_HEADS
    npair = d_model // LANES
    tq = min(tq, seq)
    nq = seq // tq
    return pl.pallas_call(
        functools.partial(_sb_kernel, tq=tq, tk=tk, hd=hd),
        grid=(batch, npair, nq),
        in_specs=[pl.BlockSpec((tq, LANES), lambda b, p, i: (b * nq + i, p)),
                  pl.BlockSpec((seq, LANES), lambda b, p, i: (b, npair + p)),
                  pl.BlockSpec((seq, LANES), lambda b, p, i: (b, 2 * npair + p))],
        out_specs=pl.BlockSpec((tq, LANES), lambda b, p, i: (b * nq + i, p)),
        out_shape=jax.ShapeDtypeStruct((batch * seq, d_model), BF16),
        scratch_shapes=[pltpu.VMEM((2, tq, LANES), F32), pltpu.VMEM((2, tq, LANES), F32)],
        compiler_params=_params("parallel", "parallel", "arbitrary"),
        name="sb_attention",
    )(qkv, qkv, qkv)


def _fox_cum_kernel(f_ref, b_ref, o_ref, *, seq, blk):
    row = lax.broadcasted_iota(jnp.int32, (blk, blk), 0)
    col = lax.broadcasted_iota(jnp.int32, (blk, blk), 1)
    tri = jnp.where(col <= row, 1.0, 0.0).astype(BF16)
    run = jnp.zeros((1, f_ref.shape[-1]), F32)
    for c in range(seq // blk):
        lf = _log_sigmoid(f_ref[0, c * blk:(c + 1) * blk, :] + b_ref[...])
        hi, mid, lo = _split3(lf)
        cs = _dot(tri, hi) + _dot(tri, mid) + _dot(tri, lo) + run
        o_ref[0, c * blk:(c + 1) * blk, :] = cs
        run = cs[blk - 1:blk, :]


def fox_cum(f_pre, b_f, blk=256):
    batch, seq, nh = f_pre.shape
    blk = min(blk, seq)
    return pl.pallas_call(
        functools.partial(_fox_cum_kernel, seq=seq, blk=blk),
        grid=(batch,),
        in_specs=[pl.BlockSpec((1, seq, nh), lambda b: (b, 0, 0)),
                  pl.BlockSpec((1, nh), lambda b: (0, 0))],
        out_specs=pl.BlockSpec((1, seq, nh), lambda b: (b, 0, 0)),
        out_shape=jax.ShapeDtypeStruct((batch, seq, nh), F32),
        compiler_params=_params("parallel"),
        name="fox_cum",
    )(f_pre, b_f)


def _fox_kernel(q_ref, k_ref, v_ref, cr_ref, o_ref, acc_ref, m_ref, *, tq, tk, hd):
    i = pl.program_id(2)
    lane, q_heads = _head_queries(q_ref, tq, hd)
    klane = lax.broadcasted_iota(jnp.int32, (tk, LANES), 1)
    in_head = (klane < hd, klane >= hd)
    acc_ref[...] = jnp.zeros_like(acc_ref)
    m_ref[...] = jnp.full_like(m_ref, -jnp.inf)

    def scores(start, lo):
        ks = k_ref[pl.ds(start, tk), :]
        return [_dot_nt(q_heads[hh][lo:], ks) for hh in range(2)]

    def finish(s, start, lo, mask):
        vs = v_ref[pl.ds(start, tk), :]
        for hh in range(2):
            vh = jnp.where(in_head[hh], vs, jnp.ones_like(vs))
            cum_row = cr_ref[0, 0, hh:hh + 1, pl.ds(start, tk)]
            z = s[hh] - cum_row
            if mask is not None:
                z = jnp.where(mask, z, -jnp.inf)
            m_prev = m_ref[hh, lo:, :]
            m_new = jnp.maximum(m_prev, jnp.max(z, axis=-1, keepdims=True))
            a = jnp.exp(m_prev - m_new)
            p = jnp.exp(z - jnp.concatenate([m_new] * (tk // LANES), axis=1)).astype(BF16)
            m_ref[hh, lo:, :] = m_new
            acc_ref[hh, lo:, :] = a * acc_ref[hh, lo:, :] + _dot(p, vh)

    _sweep(i, tq, tk, scores, finish, _diag_masks(tq, tk, inclusive=True), ascending=True)
    acc0 = acc_ref[0]
    acc1 = acc_ref[1]
    o = jnp.where(lane < hd, acc0 / acc0[:, hd:hd + 1], acc1 / acc1[:, 0:1])
    o_ref[...] = o.astype(o_ref.dtype)


def fox_attention(qkv, cum, batch, seq, d_model, tq=2048, tk=512):
    hd = d_model // ATT_HEADS
    npair = d_model // LANES
    tq = min(tq, seq)
    nq = seq // tq
    cum_row = cum.reshape(batch, seq, npair, 2).transpose(0, 2, 3, 1)
    return pl.pallas_call(
        functools.partial(_fox_kernel, tq=tq, tk=tk, hd=hd),
        grid=(batch, npair, nq),
        in_specs=[pl.BlockSpec((tq, LANES), lambda b, p, i: (b * nq + i, p)),
                  pl.BlockSpec((seq, LANES), lambda b, p, i: (b, npair + p)),
                  pl.BlockSpec((seq, LANES), lambda b, p, i: (b, 2 * npair + p)),
                  pl.BlockSpec((1, 1, 2, seq), lambda b, p, i: (b, p, 0, 0))],
        out_specs=pl.BlockSpec((tq, LANES), lambda b, p, i: (b * nq + i, p)),
        out_shape=jax.ShapeDtypeStruct((batch * seq, d_model), BF16),
        scratch_shapes=[pltpu.VMEM((2, tq, LANES), F32), pltpu.VMEM((2, tq, LANES), F32)],
        compiler_params=_params("parallel", "parallel", "arbitrary"),
        name="fox_attention",
    )(qkv, qkv, qkv, cum_row)


def _lanes(x, width):
    return x if width == LANES else jnp.concatenate([x] * (width // LANES), axis=1)


def _mlstm_kernel(q_ref, k_ref, v_ref, og_ref, gc_ref, gr_ref, o_ref, c_ref, m_ref,
                  *, nh, dk, dv, L):
    @pl.when(pl.program_id(1) == 0)
    def _():
        c_ref[...] = jnp.zeros_like(c_ref)
        m_ref[...] = jnp.zeros_like(m_ref)

    row = lax.broadcasted_iota(jnp.int32, (L, L), 0)
    col = lax.broadcasted_iota(jnp.int32, (L, L), 1)
    causal = col <= row
    scale = dk ** -0.5
    ones = jnp.ones((L, LANES), BF16)
    bb = q_ref.shape[0]
    chains = [(bi, h) for bi in range(bb) for h in range(nh)]
    n_chains = len(chains)
    q = [q_ref[bi, :, h * dk:(h + 1) * dk] for bi, h in chains]
    k = [k_ref[bi, :, h * dk:(h + 1) * dk] for bi, h in chains]
    v = [jnp.concatenate([v_ref[bi, :, h * dv:(h + 1) * dv], ones], axis=1) for bi, h in chains]

    qk_raw = [_dot_nt(q[s], k[s]) for s in range(n_chains)]
    q_state = [_dot(q[s], c_ref[s].astype(BF16)) for s in range(n_chains)]

    w_intra, w_inter, floor, m_new, decay, kw = [], [], [], [], [], []
    for s, (bi, h) in enumerate(chains):
        gcol = gc_ref[bi]
        grow = gr_ref[bi, 0]
        i_col = gcol[:, h:h + 1]
        i_row = grow[h:h + 1, :]
        lf_col = _log_sigmoid(gcol[:, nh + h:nh + h + 1])
        lf_row = _log_sigmoid(grow[nh + h:nh + h + 1, :])
        hi, lo = _split2(jnp.where(causal, lf_row, 0.0))
        bcum = _dot(hi, ones) + _dot(lo, ones)
        bcum_row = jnp.sum(jnp.where(row <= col, lf_col, 0.0), axis=0, keepdims=True)
        m_prev = m_ref[s]
        d = jnp.where(causal, bcum[:, :L] - bcum_row + i_row, -jnp.inf)
        inter = bcum + m_prev
        mt = jnp.maximum(inter, jnp.max(d, axis=1, keepdims=True))
        mn = mt[L - 1:L, 0:1]
        blast = bcum[L - 1:L, 0:1]
        m_new.append(mn)
        w_intra.append(jnp.exp(d - mt[:, :L]))
        w_inter.append(jnp.exp(inter - mt))
        floor.append(jnp.exp(-mt))
        decay.append(jnp.exp(blast + m_prev - mn))
        kw.append((k[s].astype(F32) * scale) * jnp.exp(blast - _lanes(bcum, dk) + i_col - mn))

    qk = [(qk_raw[s] * (scale * w_intra[s])).astype(BF16) for s in range(n_chains)]
    intra = [_dot(qk[s], v[s]) for s in range(n_chains)]
    kv = [_dot_tn(kw[s].astype(BF16), v[s]) for s in range(n_chains)]

    new_c, outs = [], []
    for s, (bi, h) in enumerate(chains):
        numden = _lanes(w_inter[s], dv + LANES) * q_state[s] + intra[s]
        den = jnp.maximum(jnp.abs(numden[:, dv:]), floor[s])
        h_out = numden[:, :dv] / _lanes(den, dv)
        new_c.append(decay[s] * c_ref[s] + kv[s])
        gate = _sigmoid(og_ref[bi, :, h * dv:(h + 1) * dv].astype(F32))
        outs.append((h_out * gate).astype(o_ref.dtype))
    c_ref[...] = jnp.stack(new_c)
    m_ref[...] = jnp.stack(m_new)
    o_ref[...] = jnp.stack([jnp.concatenate(outs[bi * nh:(bi + 1) * nh], axis=1) for bi in range(bb)])


def mlstm_core(proj, gates, batch, seq, d_model, bb):
    nh, L = ML_HEADS, CHUNK
    dk = d_model // 2 // nh
    dv = d_model // nh
    nc = seq // L
    qw = nh * dk
    bb = min(bb, batch)
    proj = proj.reshape(batch, seq, 3 * d_model)
    gates_row = gates.reshape(batch, nc, L, 2 * nh).transpose(0, 1, 3, 2)
    out = pl.pallas_call(
        functools.partial(_mlstm_kernel, nh=nh, dk=dk, dv=dv, L=L),
        grid=(batch // bb, nc),
        in_specs=[pl.BlockSpec((bb, L, qw), lambda g, c: (g, c, 0)),
                  pl.BlockSpec((bb, L, qw), lambda g, c: (g, c, 1)),
                  pl.BlockSpec((bb, L, d_model), lambda g, c: (g, c, 1)),
                  pl.BlockSpec((bb, L, d_model), lambda g, c: (g, c, 2)),
                  pl.BlockSpec((bb, L, 2 * nh), lambda g, c: (g, c, 0)),
                  pl.BlockSpec((bb, 1, 2 * nh, L), lambda g, c: (g, c, 0, 0))],
        out_specs=pl.BlockSpec((bb, L, d_model), lambda g, c: (g, c, 0)),
        out_shape=jax.ShapeDtypeStruct((batch, seq, d_model), BF16),
        scratch_shapes=[pltpu.VMEM((bb * nh, dk, dv + LANES), F32),
                        pltpu.VMEM((bb * nh, 1, 1), F32)],
        compiler_params=_params("parallel", "arbitrary"),
        name="mlstm_core",
    )(proj, proj, proj, proj, gates, gates_row)
    return out.reshape(batch * seq, d_model)


def _gla_kernel(q_ref, k_ref, v_ref, r_ref, a_ref, wh_ref, wl_ref, ba_ref, ng_ref, o_ref, st_ref,
                *, nh, dk, dv, L):
    @pl.when(pl.program_id(1) == 0)
    def _():
        st_ref[...] = jnp.zeros_like(st_ref)

    row = lax.broadcasted_iota(jnp.int32, (L, L), 0)
    col = lax.broadcasted_iota(jnp.int32, (L, L), 1)
    causal = col <= row
    tri = jnp.where(causal, 1.0, 0.0).astype(BF16)
    scale = dk ** -0.5
    bb = q_ref.shape[0]
    chains = [(bi, h) for bi in range(bb) for h in range(nh)]

    logits = []
    for bi in range(bb):
        a_hi, a_lo = _split2(a_ref[bi])
        wh = wh_ref[...]
        logits.append(_dot(a_hi, wh) + _dot(a_lo, wh) + _dot(a_hi, wl_ref[...]) + ba_ref[...])

    bcums = []
    for bi in range(bb):
        la_hi, la_lo = _split2(_log_sigmoid(logits[bi]) * (1.0 / GLA_TAU))
        bcums.append(_dot(tri, la_hi) + _dot(tri, la_lo))

    v, q_rel, k_rel, q_dec, k_dec, carry = [], [], [], [], [], []
    for bi, h in chains:
        hs = slice(h * dk, (h + 1) * dk)
        bcum = bcums[bi][:, hs]
        blast = bcum[L - 1:L, :]
        ref = bcum[L // 2 - 1:L // 2, :]
        qf = q_ref[bi, :, hs].astype(F32) * scale
        kf = k_ref[bi, :, hs].astype(F32)
        v.append(v_ref[bi, :, h * dv:(h + 1) * dv])
        q_rel.append((qf * jnp.exp(bcum - ref)).astype(BF16))
        k_rel.append((kf * jnp.exp(ref - bcum)).astype(BF16))
        q_dec.append((qf * jnp.exp(bcum)).astype(BF16))
        k_dec.append((kf * jnp.exp(blast - bcum)).astype(BF16))
        carry.append(jnp.exp(blast))

    n = len(chains)
    att = [jnp.where(causal, _dot_nt(q_rel[s], k_rel[s]), 0.0).astype(BF16) for s in range(n)]
    o_state = [_dot_nt(q_dec[s], st_ref[s].astype(BF16)) for s in range(n)]
    kv = [_dot_tn(v[s], k_dec[s]) for s in range(n)]

    o_intra = [_dot(att[s], v[s]) for s in range(n)]
    new_st, outs = [], []
    for s, (bi, h) in enumerate(chains):
        o = o_intra[s] + o_state[s]
        o = o * lax.rsqrt(jnp.mean(o * o, axis=-1, keepdims=True) + RMS_EPS) * ng_ref[...]
        r = r_ref[bi, :, h * dv:(h + 1) * dv].astype(F32)
        outs.append((o * (r * _sigmoid(r))).astype(o_ref.dtype))
        new_st.append(st_ref[s] * carry[s] + kv[s])
    st_ref[...] = jnp.stack(new_st)
    o_ref[...] = jnp.stack([jnp.concatenate(outs[bi * nh:(bi + 1) * nh], axis=1) for bi in range(bb)])


def gla_core(proj, a_low, wa_hi, wa_lo, b_a, norm_g, batch, seq, d_model, bb):
    nh, L = GLA_HEADS, CHUNK
    dk = d_model // 2 // nh
    dv = d_model // nh
    nc = seq // L
    qw = nh * dk
    bb = min(bb, batch)
    proj = proj.reshape(batch, seq, 3 * d_model)
    a_low = a_low.reshape(batch, seq, LANES)
    out = pl.pallas_call(
        functools.partial(_gla_kernel, nh=nh, dk=dk, dv=dv, L=L),
        grid=(batch // bb, nc),
        in_specs=[pl.BlockSpec((bb, L, qw), lambda g, c: (g, c, 0)),
                  pl.BlockSpec((bb, L, qw), lambda g, c: (g, c, 1)),
                  pl.BlockSpec((bb, L, d_model), lambda g, c: (g, c, 1)),
                  pl.BlockSpec((bb, L, d_model), lambda g, c: (g, c, 2)),
                  pl.BlockSpec((bb, L, LANES), lambda g, c: (g, c, 0)),
                  pl.BlockSpec((LANES, qw), lambda g, c: (0, 0)),
                  pl.BlockSpec((LANES, qw), lambda g, c: (0, 0)),
                  pl.BlockSpec((1, qw), lambda g, c: (0, 0)),
                  pl.BlockSpec((1, dv), lambda g, c: (0, 0))],
        out_specs=pl.BlockSpec((bb, L, d_model), lambda g, c: (g, c, 0)),
        out_shape=jax.ShapeDtypeStruct((batch, seq, d_model), BF16),
        scratch_shapes=[pltpu.VMEM((bb * nh, dv, dk), F32)],
        compiler_params=_params("parallel", "arbitrary"),
        name="gla_core",
    )(proj, proj, proj, proj, a_low, wa_hi, wa_lo, b_a, norm_g)
    return out.reshape(batch * seq, d_model)


def _mix_xattn_kernel(y_ref, hf_ref, wm_ref, g0_ref, b0_ref, wq_ref, kv_ref, wo_ref, g1_ref, b1_ref,
                      of_ref, ob_ref, *, nh, alpha):
    d = wq_ref.shape[1]
    hd = d // nh
    h1 = _layer_norm(alpha * hf_ref[...] + _dot(y_ref[...], wm_ref[...]), g0_ref[...], b0_ref[...])
    q = (_dot(h1.astype(BF16), wq_ref[...]) * (hd ** -0.5)).astype(BF16)
    outs = []
    for h in range(nh):
        kh = kv_ref[:, h * hd:(h + 1) * hd]
        vh = kv_ref[:, d + h * hd:d + (h + 1) * hd]
        s = _dot_nt(q[:, h * hd:(h + 1) * hd], kh)
        p = jnp.exp(s - jnp.max(s, axis=-1, keepdims=True))
        l = jnp.sum(p, axis=-1, keepdims=True)
        outs.append((_dot(p.astype(BF16), vh) / l).astype(BF16))
    o = jnp.concatenate(outs, axis=-1)
    h2 = _layer_norm(alpha * h1 + _dot(o, wo_ref[...]), g1_ref[...], b1_ref[...])
    of_ref[...] = h2
    ob_ref[...] = h2.astype(BF16)


def mix_xattn_block(y, hf, w_mix, g0, b0, wq, kv, wo, g1, b1, layer, batch, seq, n_mem, alpha, tm):
    d = hf.shape[1]
    tm = min(tm, seq)
    nt = seq // tm
    rows = pl.BlockSpec((tm, d), lambda bi, i: (bi * nt + i, 0))
    whole = lambda shape: pl.BlockSpec(shape, lambda bi, i: (0, 0), pipeline_mode=pl.Buffered(1))
    weight = pl.BlockSpec((None, d, d), lambda bi, i: (layer, 0, 0), pipeline_mode=pl.Buffered(1))
    return pl.pallas_call(
        functools.partial(_mix_xattn_kernel, nh=XA_HEADS, alpha=alpha),
        grid=(batch, nt),
        in_specs=[rows, rows, weight, whole((1, d)), whole((1, d)),
                  weight, pl.BlockSpec((n_mem, 2 * d), lambda bi, i: (bi, 0)), weight,
                  whole((1, d)), whole((1, d))],
        out_specs=[rows, rows],
        out_shape=[jax.ShapeDtypeStruct((batch * seq, d), F32),
                   jax.ShapeDtypeStruct((batch * seq, d), BF16)],
        compiler_params=_params("parallel", "parallel"),
        name="mix_xattn_block",
    )(y, hf, w_mix, g0, b0, wq, kv, wo, g1, b1)


def _gelu_tanh(x):
    return 0.5 * x * (1.0 + jnp.tanh(0.7978845608028654 * (x + 0.044715 * (x * x * x))))


def _ffn_kernel(hb_ref, hf_ref, wu_ref, cw_ref, cb_ref, wd_ref, g_ref, b_ref, of_ref, ob_ref, tail_ref,
                *, tm, tf, group, blocks_per_seq, alpha):
    i = pl.program_id(0)
    f = wd_ref.shape[0]

    @pl.when(i % blocks_per_seq == 0)
    def _():
        tail_ref[...] = jnp.zeros_like(tail_ref)

    hb = hb_ref[...]
    row = lax.broadcasted_iota(jnp.int32, (SUBLANES, tf), 0)

    def up(c):
        return [_dot(hb, wu_ref[:, lo:lo + tf]) for lo in (c * tf, f + c * tf)]

    def conv(u, lo):
        cols = slice(lo, lo + tf)
        tail = tail_ref[:, cols]
        tail_ref[:, cols] = u[tm - SUBLANES:, :]
        p1 = tail[SUBLANES - 1:SUBLANES, :]
        p2 = tail[SUBLANES - 2:SUBLANES - 1, :]
        r1 = pltpu.roll(u, 1, 0)
        r2 = pltpu.roll(u, 2, 0)
        top1 = jnp.where(row == 0, p1, r1[:SUBLANES])
        top2 = jnp.where(row == 0, p2, jnp.where(row == 1, p1, r2[:SUBLANES]))
        u1 = jnp.concatenate([top1, r1[SUBLANES:]], axis=0)
        u2 = jnp.concatenate([top2, r2[SUBLANES:]], axis=0)
        c = cw_ref[:, cols]
        return c[0:1, :] * u2 + c[1:2, :] * u1 + c[2:3, :] * u + cb_ref[:, cols]

    n = f // tf
    y = alpha * hf_ref[...]
    u_next = up(0)
    acts, first = [], 0
    for c in range(n):
        u_gate, u_val = u_next
        if c + 1 < n:
            u_next = up(c + 1)
        acts.append((_gelu_tanh(conv(u_gate, c * tf)) * conv(u_val, f + c * tf)).astype(BF16))
        if len(acts) == group or c + 1 == n:
            y = y + _dot(jnp.concatenate(acts, axis=1), wd_ref[first * tf:(c + 1) * tf, :])
            acts, first = [], c + 1
    y = _layer_norm(y, g_ref[...], b_ref[...])
    of_ref[...] = y
    ob_ref[...] = y.astype(BF16)


def ffn_block(hb, hf, w_up, conv_w, conv_b, w_down, g, b, layer, seq, alpha, tm, tf, group):
    t, d = hb.shape
    f = w_down.shape[1]
    tm = min(tm, seq)
    whole = lambda shape: pl.BlockSpec(shape, lambda i: (0, 0), pipeline_mode=pl.Buffered(1))
    weight = lambda shape: pl.BlockSpec((None,) + shape, lambda i: (layer, 0, 0), pipeline_mode=pl.Buffered(1))
    return pl.pallas_call(
        functools.partial(_ffn_kernel, tm=tm, tf=tf, group=group, blocks_per_seq=seq // tm, alpha=alpha),
        grid=(t // tm,),
        in_specs=[pl.BlockSpec((tm, d), lambda i: (i, 0)),
                  pl.BlockSpec((tm, d), lambda i: (i, 0)),
                  weight((d, 2 * f)),
                  whole((CONV_W, 2 * f)),
                  whole((1, 2 * f)),
                  weight((f, d)),
                  whole((1, d)),
                  whole((1, d))],
        out_specs=[pl.BlockSpec((tm, d), lambda i: (i, 0)),
                   pl.BlockSpec((tm, d), lambda i: (i, 0))],
        out_shape=[jax.ShapeDtypeStruct((t, d), F32),
                   jax.ShapeDtypeStruct((t, d), BF16)],
        scratch_shapes=[pltpu.VMEM((SUBLANES, 2 * f), F32)],
        compiler_params=_params("arbitrary"),
        name="ffn_block",
    )(hb, hf, w_up, conv_w, conv_b, w_down, g, b)


def _pad_cols(w, width):
    return jnp.pad(w, ((0, 0), (0, width - w.shape[1])))


def kernel(x, mem, ln_g, ln_b, mix_wo, sb_win, fox_win, fox_bf, ml_win, ml_bi, ml_bf,
           gla_win, gla_wa2, gla_ba, gla_norm_g, xa_wq, xa_wkv, xa_wo,
           ffn_up, ffn_conv, ffn_conv_b, ffn_down):
    batch, seq, d = x.shape
    n_mem = mem.shape[1]
    depth = mix_wo.shape[0]
    t = batch * seq
    alpha = (2.0 * depth) ** 0.25
    main = 3 * d

    hf = x.reshape(t, d)
    hb = hf
    memb = mem.reshape(batch * n_mem, d).astype(BF16)
    in_w = [w.astype(BF16) for w in (sb_win, fox_win, ml_win, gla_win)]
    mix_wo_b, xa_wq_b, xa_wkv_b, xa_wo_b = (w.astype(BF16) for w in (mix_wo, xa_wq, xa_wkv, xa_wo))
    ffn_up_b, ffn_down_b = ffn_up.astype(BF16), ffn_down.astype(BF16)

    for layer in range(depth):
        kind = layer % 4
        occ = layer // 4
        if kind == 0:
            proj = linear(hb, in_w[0], occ, main, BF16, tm=1024, tn=1024)
            y = sb_attention(proj, batch, seq, d)
        else:
            w_narrow = (fox_win, ml_win, gla_win)[kind - 1][occ][:, main:]
            proj, narrow = in_proj(hb, in_w[kind], occ, main, _pad_cols(w_narrow, LANES).astype(BF16),
                                   tm=1024, tn=1024)
        if kind == 1:
            cum = fox_cum(narrow.reshape(batch, seq, LANES), _pad_cols(fox_bf[occ][None, :], LANES))
            y = fox_attention(proj, cum[:, :, :ATT_HEADS], batch, seq, d)
        elif kind == 2:
            gates = narrow[:, :2 * ML_HEADS] + jnp.concatenate([ml_bi[occ], ml_bf[occ]])[None, :]
            y = mlstm_core(proj, gates.reshape(batch, seq, 2 * ML_HEADS), batch, seq, d, bb=8)
        elif kind == 3:
            wa = jnp.pad(gla_wa2[occ], ((0, LANES - GLA_RANK), (0, 0)))
            wa_hi = wa.astype(BF16)
            wa_lo = (wa - wa_hi.astype(F32)).astype(BF16)
            y = gla_core(proj, narrow, wa_hi, wa_lo, gla_ba[occ][None, :], gla_norm_g[occ][None, :],
                         batch, seq, d, bb=8)
        kv = linear(memb, xa_wkv_b, layer, 2 * d, BF16, tm=1024, tn=1024)
        hf, hb = mix_xattn_block(y, hf, mix_wo_b, ln_g[layer, 0][None, :], ln_b[layer, 0][None, :],
                                 xa_wq_b, kv, xa_wo_b, ln_g[layer, 1][None, :], ln_b[layer, 1][None, :],
                                 layer, batch, seq, n_mem, alpha, tm=512)
        hf, hb = ffn_block(hb, hf, ffn_up_b, ffn_conv[layer, :, 0, :], ffn_conv_b[layer][None, :], ffn_down_b,
                           ln_g[layer, 2][None, :], ln_b[layer, 2][None, :],
                           layer, seq, alpha, tm=512, tf=256, group=4)
    return hf.reshape(batch, seq, d)
```
